```python
import math
import jax
import jax.numpy as jnp
from jax import lax
import numpy as np

D_MODEL = 1024
BATCH = 2
SEQ = 16384
DEPTH = 2

GRID_W = 64
CTX_LEN = 256
A_HEADS = 8
A_DIM = 64
A_WIDTH = A_HEADS * 2 * A_DIM
A_QBLOCK = 128
ROPE_THETA = 10000.0
M_HEADS = 4
M_QK = 128
M_V = 256
M_QKW = M_HEADS * M_QK
M_VW = M_HEADS * M_V
M_CHUNK = 64
CONV_K = 3
F_BIAS_LO = 3.0
F_BIAS_HI = 6.0
N_EXPERTS = 256
TOP_K = 8
E_HIDDEN = 256
S_HIDDEN = 256
ROUTE_SCALE = 2.5
MOE_BLOCK = 128
ALPHA = (2 * DEPTH) ** 0.25
BETA = (8 * DEPTH) ** -0.25
LN_EPS = 1e-5
RMS_EPS = 1e-6
IN_WIDTHS = (A_WIDTH, A_WIDTH, A_WIDTH, 2 * M_QKW, M_VW, M_VW, 4 * M_HEADS, 2 * D_MODEL)
IN_SPLITS = tuple(int(v) for v in np.cumsum(IN_WIDTHS)[:-1])
IN_WIDTH = sum(IN_WIDTHS)

kernel_name = 'hybrid_diffattn_mlstm_moe_dit'


def layer_norm(x, g, b):
    xf = x.astype(jnp.float32)
    mu = jnp.mean(xf, axis=-1, keepdims=True)
    var = jnp.mean(jnp.square(xf - mu), axis=-1, keepdims=True)
    y = (xf - mu) * lax.rsqrt(var + LN_EPS)
    return (y * g.astype(jnp.float32) + b.astype(jnp.float32)).astype(x.dtype)


def rms_norm(x, g):
    xf = x.astype(jnp.float32)
    y = xf * lax.rsqrt(jnp.mean(xf * xf, axis=-1, keepdims=True) + RMS_EPS)
    return (y * g.astype(jnp.float32)).astype(x.dtype)


def modulate(h, shift, scale):
    return h * (1.0 + scale) + shift


def centred_conv(x, w):
    L = x.shape[1]
    pad = CONV_K // 2
    xp = jnp.pad(x, ((0, 0), (pad, pad), (0, 0)))
    y = xp[:, 0:L] * w[0]
    for j in range(1, CONV_K):
        y = y + xp[:, j:j + L] * w[j]
    return y


def axial_rope_tables(rows, dtype):
    n_tok = rows * GRID_W
    row = jnp.repeat(jnp.arange(rows, dtype=jnp.float32), GRID_W)
    col = jnp.tile(jnp.arange(GRID_W, dtype=jnp.float32), rows)
    n_freq = A_DIM // 4
    inv = jnp.power(ROPE_THETA, -jnp.arange(n_freq, dtype=jnp.float32) / n_freq)
    shape = (1, n_tok, 1, 1, n_freq)
    ang_r = (row[:, None] * inv).reshape(shape)
    ang_c = (col[:, None] * inv).reshape(shape)
    return (jnp.cos(ang_r).astype(dtype), jnp.sin(ang_r).astype(dtype),
            jnp.cos(ang_c).astype(dtype), jnp.sin(ang_c).astype(dtype))


def rotate(p, cos, sin):
    p1, p2 = jnp.split(p, 2, axis=-1)
    return jnp.concatenate([p1 * cos - p2 * sin, p2 * cos + p1 * sin], axis=-1)


def apply_axial_rope(x, cos_r, sin_r, cos_c, sin_c):
    half = A_DIM // 2
    return jnp.concatenate([rotate(x[..., :half], cos_r, sin_r),
                            rotate(x[..., half:], cos_c, sin_c)], axis=-1)


def project_heads(h, p):
    B, L, _ = h.shape
    z = h @ p['w_in']
    qa, ka, va, qkm, vm, om, gm, gmerge = jnp.split(z, IN_SPLITS, axis=-1)
    qkm = jax.nn.silu(centred_conv(qkm, p['conv_qk']))
    qm, km = jnp.split(qkm, 2, axis=-1)

    def heads(a, d):
        return a.reshape(B, L, M_HEADS, d).transpose(0, 2, 1, 3).astype(jnp.float32)

    gates = (gm.astype(jnp.float32) + p['b_gates'].astype(jnp.float32)).reshape(B, L, 4, M_HEADS).transpose(2, 0, 3, 1)
    return (qa.reshape(B, L, A_HEADS, 2, A_DIM), ka.reshape(B, L, A_HEADS, 2, A_DIM),
            va.reshape(B, L, A_HEADS, 2 * A_DIM),
            heads(qm, M_QK), heads(km, M_QK) * (M_QK ** -0.5), heads(vm, M_V),
            om, gates, gmerge)


def diff_attend(q, keys, vals, lam_val):
    s = jnp.einsum('bqhjd,bkhjd->bhjqk', q, keys, preferred_element_type=jnp.float32) * (A_DIM ** -0.5)
    prob = jax.nn.softmax(s, axis=-1)
    w = (prob[:, :, 0] - lam_val * prob[:, :, 1]).astype(vals.dtype)
    return jnp.einsum('bhqk,bkhe->bqhe', w, vals)


def diff_attention(q_l, k_l, v_l, q_c, k_c, v_c, lam_val, lam_init, g, with_ctx_out):
    B, S = q_l.shape[:2]
    keys = jnp.concatenate([k_l, k_c], axis=1)
    vals = jnp.concatenate([v_l, v_c], axis=1)
    nb = S // A_QBLOCK
    qb = q_l.reshape(B, nb, A_QBLOCK, A_HEADS, 2, A_DIM).swapaxes(0, 1)
    o_l = lax.map(lambda qq: diff_attend(qq, keys, vals, lam_val), qb)
    o_l = o_l.swapaxes(0, 1).reshape(B, S, A_HEADS, 2 * A_DIM)

    def post(o):
        return (rms_norm(o, g) * (1.0 - lam_init)).reshape(o.shape[0], o.shape[1], A_WIDTH)

    o_c = post(diff_attend(q_c, k_c, v_c, lam_val)) if with_ctx_out else None
    return post(o_l), o_c


def mlstm_scan(q, k, v, ig, fg, state, with_out):
    B, H, L, _ = q.shape
    nc = L // M_CHUNK

    def chunks(a):
        return jnp.moveaxis(a.reshape((B, H, nc, M_CHUNK) + a.shape[3:]), 2, 0)

    tri = jnp.tril(jnp.ones((M_CHUNK, M_CHUNK), dtype=bool))

    def step(carry, inp):
        C, n, m = carry
        qc, kc, vc, ic, fc = inp
        b = jnp.cumsum(jax.nn.log_sigmoid(fc), axis=-1)
        b_end = b[..., -1]
        g = b_end[..., None] - b + ic
        m_new = jnp.maximum(b_end + m, jnp.max(g, axis=-1))
        w = jnp.exp(g - m_new[..., None])
        decay = jnp.exp(b_end + m - m_new)
        C_new = decay[..., None, None] * C + jnp.einsum('bhs,bhsd,bhse->bhde', w, kc, vc)
        n_new = decay[..., None] * n + jnp.einsum('bhs,bhsd->bhd', w, kc)
        if not with_out:
            return (C_new, n_new, m_new), None
        dmat = jnp.where(tri, b[..., :, None] - b[..., None, :] + ic[..., None, :], -jnp.inf)
        inter = b + m[..., None]
        m_t = jnp.maximum(inter, jnp.max(dmat, axis=-1))
        dexp = jnp.exp(dmat - m_t[..., None])
        iscale = jnp.exp(inter - m_t)
        s = jnp.einsum('bhtd,bhsd->bhts', qc, kc) * dexp
        num = iscale[..., None] * jnp.einsum('bhtd,bhde->bhte', qc, C) + jnp.einsum('bhts,bhse->bhte', s, vc)
        den = iscale * jnp.einsum('bhtd,bhd->bht', qc, n) + jnp.sum(s, axis=-1)
        h = num / jnp.maximum(jnp.abs(den), jnp.exp(-m_t))[..., None]
        return (C_new, n_new, m_new), h

    state, hs = lax.scan(step, state, (chunks(q), chunks(k), chunks(v), chunks(ig), chunks(fg)))
    if not with_out:
        return None, state
    return jnp.moveaxis(hs, 0, 2).reshape(B, H, L, M_V), state


def mlstm_bidirectional(q_l, k_l, v_l, g_l, q_c, k_c, v_c, g_c, with_ctx_out):
    B = q_l.shape[0]
    init = (jnp.zeros((B, M_HEADS, M_QK, M_V), jnp.float32),
            jnp.zeros((B, M_HEADS, M_QK), jnp.float32),
            jnp.zeros((B, M_HEADS), jnp.float32))

    def flip(a):
        return jnp.flip(a, axis=2)

    h_cf, st_f = mlstm_scan(q_c, k_c, v_c, g_c[0], g_c[1], init, with_ctx_out)
    h_lf, _ = mlstm_scan(q_l, k_l, v_l, g_l[0], g_l[1], st_f, True)
    h_cb, st_b = mlstm_scan(flip(q_c), flip(k_c), flip(v_c), flip(g_c[2]), flip(g_c[3]), init, with_ctx_out)
    h_lb, _ = mlstm_scan(flip(q_l), flip(k_l), flip(v_l), flip(g_l[2]), flip(g_l[3]), st_b, True)
    h_lat = h_lf + flip(h_lb)
    h_ctx = h_cf + flip(h_cb) if with_ctx_out else None
    return h_lat, h_ctx


def merge_branches(o_a, h_m, o_m, g_merge, p):
    B, L = o_m.shape[:2]
    hm = rms_norm(h_m.transpose(0, 2, 1, 3), p['norm_m'].reshape(M_HEADS, M_V)).reshape(B, L, M_VW)
    hm = hm.astype(o_m.dtype) * jax.nn.sigmoid(o_m)
    ga, gb = jnp.split(jax.nn.sigmoid(g_merge), 2, axis=-1)
    y = ga * (o_a @ p['w_br_a']) + gb * (hm @ p['w_br_m'])
    return y @ p['w_out']


def token_mixer(h_lat, h_ctx, rope, p, lam_init, with_ctx_out):
    qa_l, ka_l, va_l, qm_l, km_l, vm_l, om_l, gm_l, gmg_l = project_heads(h_lat, p)
    qa_c, ka_c, va_c, qm_c, km_c, vm_c, om_c, gm_c, gmg_c = project_heads(h_ctx, p)
    qa_l = apply_axial_rope(qa_l, *rope)
    ka_l = apply_axial_rope(ka_l, *rope)
    lq1, lk1, lq2, lk2 = p['lam'].astype(jnp.float32)
    lam_val = jnp.exp(jnp.sum(lq1 * lk1)) - jnp.exp(jnp.sum(lq2 * lk2)) + lam_init
    oa_l, oa_c = diff_attention(qa_l, ka_l, va_l, qa_c, ka_c, va_c, lam_val, lam_init, p['subln_a'], with_ctx_out)
    hm_l, hm_c = mlstm_bidirectional(qm_l, km_l, vm_l, gm_l, qm_c, km_c, vm_c, gm_c, with_ctx_out)
    y_l = merge_branches(oa_l, hm_l, om_l, gmg_l, p)
    y_c = merge_branches(oa_c, hm_c, om_c, gmg_c, p) if with_ctx_out else None
    return y_l, y_c


def moe_ffn(h, w_router, e_bias, w_e_in, w_e_out, ws_in, ws_out):
    T, D = h.shape
    scores = jax.nn.sigmoid((h @ w_router).astype(jnp.float32))
    _, idx = lax.top_k(scores + e_bias.astype(jnp.float32), TOP_K)
    w = jnp.take_along_axis(scores, idx, axis=-1)
    w = w / jnp.sum(w, axis=-1, keepdims=True) * ROUTE_SCALE
    A = T * TOP_K
    n_blk = -(-A // MOE_BLOCK) + N_EXPERTS
    flat_e = idx.reshape(-1)
    flat_t = jnp.repeat(jnp.arange(T, dtype=jnp.int32), TOP_K)
    order = jnp.argsort(flat_e)
    se = flat_e[order]
    counts = jnp.bincount(flat_e, length=N_EXPERTS)
    padded = (counts + MOE_BLOCK - 1) // MOE_BLOCK * MOE_BLOCK
    start = jnp.cumsum(counts) - counts
    pend = jnp.cumsum(padded)
    pstart = pend - padded
    dest = pstart[se] + (jnp.arange(A) - start[se])
    buf_t = jnp.full((n_blk * MOE_BLOCK,), T, jnp.int32).at[dest].set(flat_t[order])
    buf_w = jnp.zeros((n_blk * MOE_BLOCK,), h.dtype).at[dest].set(w.reshape(-1)[order].astype(h.dtype))
    blk_e = jnp.minimum(jnp.searchsorted(pend, jnp.arange(n_blk) * MOE_BLOCK, side='right'), N_EXPERTS - 1)

    def expert_block(args):
        tb, e = args
        xb = h[jnp.minimum(tb, T - 1)]
        a, g = jnp.split(xb @ w_e_in[e], 2, axis=-1)
        return (jax.nn.silu(a) * g) @ w_e_out[e]

    y_buf = lax.map(expert_block, (buf_t.reshape(n_blk, MOE_BLOCK), blk_e)).reshape(n_blk * MOE_BLOCK, D)
    routed = jnp.zeros((T, D), h.dtype).at[buf_t].add(y_buf * buf_w[:, None], mode='drop')
    a, g = jnp.split(h @ ws_in, 2, axis=-1)
    return routed + (jax.nn.silu(a) * g) @ ws_out


def setup_inputs(seed: int = 0) -> dict:
    key = jax.random.key(seed)
    ks = jax.random.split(key, 26)
    L, D, E = DEPTH, D_MODEL, N_EXPERTS

    def nrm(k, shape, std):
        return jax.random.normal(k, shape, jnp.float32) * std

    f_bias = jnp.linspace(F_BIAS_LO, F_BIAS_HI, M_HEADS, dtype=jnp.float32)
    zero_b = jnp.zeros_like(f_bias)
    b_gates = (nrm(ks[7], (L, 4, M_HEADS), 0.1) + jnp.stack([zero_b, f_bias, zero_b, f_bias])).reshape(L, 4 * M_HEADS)
    return {
        'x': nrm(ks[0], (BATCH, SEQ, D), 1.0),
        'c': nrm(ks[1], (BATCH, D), 1.0),
        'ctx': nrm(ks[2], (BATCH, CTX_LEN, D), 1.0),
        'c_ctx': nrm(ks[3], (D,), 1.0),
        'ada_w': nrm(ks[4], (L, D, 6 * D), 0.5 * D ** -0.5),
        'ada_b': nrm(ks[5], (L, 6 * D), 0.02),
        'w_in': nrm(ks[6], (L, D, IN_WIDTH), D ** -0.5),
        'b_gates': b_gates,
        'conv_qk': nrm(ks[8], (L, CONV_K, 2 * M_QKW), CONV_K ** -0.5),
        'lam': nrm(ks[9], (L, 4, A_DIM), 0.1),
        'subln_a': 1.0 + nrm(ks[10], (L, 2 * A_DIM), 0.02),
        'norm_m': 1.0 + nrm(ks[11], (L, M_VW), 0.02),
        'w_br_a': nrm(ks[12], (L, A_WIDTH, D), A_WIDTH ** -0.5),
        'w_br_m': nrm(ks[13], (L, M_VW, D), M_VW ** -0.5),
        'w_out': nrm(ks[14], (L, D, D), BETA * D ** -0.5),
        'ln1_g': 1.0 + nrm(ks[15], (L, D), 0.02),
        'ln1_b': nrm(ks[16], (L, D), 0.02),
        'ln2_g': 1.0 + nrm(ks[17], (L, D), 0.02),
        'ln2_b': nrm(ks[18], (L, D), 0.02),
        'w_router': nrm(ks[19], (L, D, E), D ** -0.5),
        'e_bias': nrm(ks[20], (L, E), 0.01),
        'w_e_in': nrm(ks[21], (L, E, D, 2 * E_HIDDEN), D ** -0.5),
        'w_e_out': nrm(ks[22], (L, E, E_HIDDEN, D), BETA * E_HIDDEN ** -0.5),
        'ws_in': nrm(ks[23], (L, D, 2 * S_HIDDEN), D ** -0.5),
        'ws_out': nrm(ks[24], (L, S_HIDDEN, D), BETA * S_HIDDEN ** -0.5),
    }


def reference(x, c, ctx, c_ctx, ada_w, ada_b, w_in, b_gates, conv_qk, lam, subln_a, norm_m,
              w_br_a, w_br_m, w_out, ln1_g, ln1_b, ln2_g, ln2_b,
              w_router, e_bias, w_e_in, w_e_out, ws_in, ws_out):
    B, S, D = x.shape
    ROWS = S // GRID_W
    rope = axial_rope_tables(ROWS, x.dtype)
    cond_lat = jax.nn.silu(c)
    cond_ctx = jax.nn.silu(c_ctx)
    s = ctx
    for l in range(DEPTH):
        last = l == DEPTH - 1
        p = {'w_in': w_in[l], 'b_gates': b_gates[l], 'conv_qk': conv_qk[l], 'lam': lam[l],
             'subln_a': subln_a[l], 'norm_m': norm_m[l], 'w_br_a': w_br_a[l], 'w_br_m': w_br_m[l],
             'w_out': w_out[l]}
        lam_init = 0.8 - 0.6 * math.exp(-0.3 * l)
        m_lat = jnp.split((cond_lat @ ada_w[l] + ada_b[l])[:, None, :], 6, axis=-1)
        m_ctx = jnp.split(cond_ctx @ ada_w[l] + ada_b[l], 6, axis=-1)
        y_l, y_c = token_mixer(modulate(x, m_lat[0], m_lat[1]), modulate(s, m_ctx[0], m_ctx[1]),
                               rope, p, lam_init, not last)
        x = layer_norm(ALPHA * x + m_lat[2] * y_l, ln1_g[l], ln1_b[l])
        h_l = modulate(x, m_lat[3], m_lat[4]).reshape(B * S, D)
        if last:
            f_l = moe_ffn(h_l, w_router[l], e_bias[l], w_e_in[l], w_e_out[l], ws_in[l], ws_out[l])
        else:
            s = layer_norm(ALPHA * s + m_ctx[2] * y_c, ln1_g[l], ln1_b[l])
            h_c = modulate(s, m_ctx[3], m_ctx[4]).reshape(-1, D)
            f = moe_ffn(jnp.concatenate([h_l, h_c], axis=0),
                        w_router[l], e_bias[l], w_e_in[l], w_e_out[l], ws_in[l], ws_out[l])
            f_l = f[:B * S]
            s = layer_norm(ALPHA * s + m_ctx[5] * f[B * S:].reshape(s.shape), ln2_g[l], ln2_b[l])
        x = layer_norm(ALPHA * x + m_lat[5] * f_l.reshape(B, S, D), ln2_g[l], ln2_b[l])
    return x
```

```python
import functools
import math

import jax
import jax.numpy as jnp
from jax import lax
from jax.experimental import pallas as pl
from jax.experimental.pallas import tpu as pltpu

F32 = jnp.float32
BF16 = jnp.bfloat16

D_MODEL = 1024
GRID_W = 64
A_HEADS = 8
A_DIM = 64
A_VDIM = 2 * A_DIM
ROPE_THETA = 10000.0
M_HEADS = 4
M_QK = 128
M_V = 256
M_VAUG = M_V + 128
N_EXPERTS = 256
TOP_K = 8
E_HIDDEN = 256
S_HIDDEN = 256
ROUTE_SCALE = 2.5
LN_EPS = 1e-5
RMS_EPS = 1e-6
N_MOD = 6
COL_TILE = 1024
N_COL_TILES = 8
MLSTM_CHUNK = 256
MOE_ROWS = 256
TOK_TILE = 256
VMEM_LIMIT = 56 * 1024 * 1024


def _cparams(n_axes):
    return pltpu.CompilerParams(dimension_semantics=("arbitrary",) * n_axes,
                                vmem_limit_bytes=VMEM_LIMIT)


def _dot(a, b):
    return jnp.dot(a, b, preferred_element_type=F32)


def _dot_nt(a, b):
    return lax.dot_general(a, b, (((1,), (1,)), ((), ())), preferred_element_type=F32)


def _split_bf16(a):
    hi = a.astype(BF16)
    lo = (a - hi.astype(F32)).astype(BF16)
    return hi, lo


def _dot3(a, b):
    ah, al = _split_bf16(a)
    bh, bl = _split_bf16(b)
    return _dot(ah, bh) + _dot(al, bh) + _dot(ah, bl)


def _sigmoid(v):
    return 1.0 / (1.0 + jnp.exp(-v))


def _log_sigmoid(v):
    return jnp.minimum(v, 0.0) - jnp.log(1.0 + jnp.exp(-jnp.abs(v)))


def _layer_norm(u, g, b):
    mu = jnp.mean(u, axis=-1, keepdims=True)
    var = jnp.mean(jnp.square(u - mu), axis=-1, keepdims=True)
    return (u - mu) * lax.rsqrt(var + LN_EPS) * g + b


def _pick_mod(is_ctx, mc_ref, ml_ref, k):
    return jnp.where(is_ctx, mc_ref[k:k + 1, :], ml_ref[0, k:k + 1, :])


def _ada_kernel(cond_ref, w_ref, b_ref, o_ref):
    c = cond_ref[...]
    c = c * _sigmoid(c)
    o_ref[0] = _dot3(c, w_ref[0]) + b_ref[0]


def _ada(cond, ada_w, ada_b):
    depth, d, width = ada_w.shape
    tn = 512
    return pl.pallas_call(
        _ada_kernel,
        grid=(depth, width // tn),
        in_specs=[pl.BlockSpec((8, d), lambda l, j: (0, 0)),
                  pl.BlockSpec((1, d, tn), lambda l, j: (l, 0, j)),
                  pl.BlockSpec((1, 1, tn), lambda l, j: (l, 0, j))],
        out_specs=pl.BlockSpec((1, 8, tn), lambda l, j: (l, 0, j)),
        out_shape=jax.ShapeDtypeStruct((depth, 8, width), F32),
        compiler_params=_cparams(2),
        name="ada",
    )(cond, ada_w, ada_b.reshape(depth, 1, width))


def _proj_kernel(x_ref, xp_ref, xn_ref, ml_ref, mc_ref, w_ref, wg_ref, bg_ref, cos_ref, sin_ref,
                 conv_ref, z_ref, gates_ref, h_scr, halo_scr, *, tm, n_lat, n_tok):
    i = pl.program_id(1)
    j = pl.program_id(2)
    row0 = i * tm

    def modulate(xv, rows):
        is_ctx = rows >= n_lat
        return xv * (1.0 + _pick_mod(is_ctx, mc_ref, ml_ref, 1)) + _pick_mod(is_ctx, mc_ref, ml_ref, 0)

    @pl.when(j == 0)
    def _():
        rows = row0 + lax.broadcasted_iota(jnp.int32, (tm, 1), 0)
        h = modulate(x_ref[0], rows)
        h_scr[...] = h.astype(BF16)
        r8 = lax.broadcasted_iota(jnp.int32, (8, 1), 0)
        halo_scr[0:8, :] = modulate(xp_ref[0], row0 - 8 + r8)
        halo_scr[8:16, :] = modulate(xn_ref[0], row0 + tm + r8)
        gates_ref[0] = _dot3(h, wg_ref[...]) + bg_ref[...]

    zt = _dot(h_scr[...], w_ref[...])

    @pl.when(j <= 1)
    def _():
        lane = lax.broadcasted_iota(jnp.int32, (tm, 128), 1)
        first = (lane % 32) < 16
        scale = jnp.where(j == 0, A_DIM ** -0.5, 1.0).astype(F32)
        cos = cos_ref[...] * scale
        sin = sin_ref[...] * scale
        for g in range(COL_TILE // 128):
            zg = zt[:, g * 128:(g + 1) * 128]
            partner = jnp.where(first, pltpu.roll(zg, 128 - 16, 1), pltpu.roll(zg, 16, 1))
            z_ref[0, :, g * 128:(g + 1) * 128] = (zg * cos + partner * sin).astype(BF16)

    @pl.when(j == 3)
    def _():
        zh = _dot(halo_scr[...].astype(BF16), w_ref[...])
        loc = lax.broadcasted_iota(jnp.int32, (tm, 1), 0)
        rows = row0 + loc
        seg_start = (rows == 0) | (rows == n_lat)
        seg_end = (rows == n_lat - 1) | (rows == n_tok - 1)
        zd = jnp.where(loc == 0, zh[7:8, :], pltpu.roll(zt, 1, 0))
        zd = jnp.where(seg_start, 0.0, zd)
        zu = jnp.where(loc == tm - 1, zh[8:9, :], pltpu.roll(zt, tm - 1, 0))
        zu = jnp.where(seg_end, 0.0, zu)
        y = zd * conv_ref[0:1, :] + zt * conv_ref[1:2, :] + zu * conv_ref[2:3, :]
        y = y * _sigmoid(y)
        lane = lax.broadcasted_iota(jnp.int32, (1, COL_TILE), 1)
        y = y * jnp.where(lane >= M_HEADS * M_QK, M_QK ** -0.5, 1.0).astype(F32)
        z_ref[0] = y.astype(BF16)

    @pl.when((j == 2) | (j >= 4))
    def _():
        z_ref[0] = zt.astype(BF16)


def _proj(xc, mod_lat, mod_ctx, w_main, w_gates, b_gates, cos_t, sin_t, conv_w, *, n_lat, tm):
    bsz, n_tok, d = xc.shape
    nt8 = n_tok // 8
    kern = functools.partial(_proj_kernel, tm=tm, n_lat=n_lat, n_tok=n_tok)
    return pl.pallas_call(
        kern,
        grid=(bsz, n_tok // tm, N_COL_TILES),
        in_specs=[
            pl.BlockSpec((1, tm, d), lambda b, i, j: (b, i, 0)),
            pl.BlockSpec((1, 8, d), lambda b, i, j: (b, jnp.maximum(i * (tm // 8) - 1, 0), 0)),
            pl.BlockSpec((1, 8, d), lambda b, i, j: (b, jnp.minimum((i + 1) * (tm // 8), nt8 - 1), 0)),
            pl.BlockSpec((1, 8, d), lambda b, i, j: (b, 0, 0)),
            pl.BlockSpec((8, d), lambda b, i, j: (0, 0)),
            pl.BlockSpec((d, COL_TILE), lambda b, i, j: (0, j)),
            pl.BlockSpec((d, 128), lambda b, i, j: (0, 0)),
            pl.BlockSpec((1, 128), lambda b, i, j: (0, 0)),
            pl.BlockSpec((tm, 128), lambda b, i, j: (i, 0)),
            pl.BlockSpec((tm, 128), lambda b, i, j: (i, 0)),
            pl.BlockSpec((8, COL_TILE), lambda b, i, j: (0, 0)),
        ],
        out_specs=[
            pl.BlockSpec((1, tm, COL_TILE), lambda b, i, j: (b, i, j)),
            pl.BlockSpec((1, tm, 128), lambda b, i, j: (b, i, 0)),
        ],
        out_shape=[jax.ShapeDtypeStruct((bsz, n_tok, N_COL_TILES * COL_TILE), BF16),
                   jax.ShapeDtypeStruct((bsz, n_tok, 128), F32)],
        scratch_shapes=[pltpu.VMEM((tm, d), BF16), pltpu.VMEM((16, d), F32)],
        compiler_params=_cparams(3),
        name="proj",
    )(xc, xc, xc, mod_lat, mod_ctx, w_main, w_gates, b_gates, cos_t, sin_t, conv_w)


def _attn_kernel(lam_ref, g_ref, q_ref, k_ref, v_ref, o_ref, *, tq, ck, n_chunks, tail_lo, tail_len, lam_init):
    q = q_ref[0]
    lane = lax.broadcasted_iota(jnp.int32, (1, A_VDIM), 1)
    zero = jnp.zeros_like(q)
    qs = (jnp.where(lane < A_DIM, q, zero), jnp.where(lane >= A_DIM, q, zero))

    def chunk(start, size, carry):
        k = k_ref[0, pl.ds(start, size), :]
        v = v_ref[0, pl.ds(start, size), :]
        out = []
        for qj, (m, l, acc) in zip(qs, carry):
            s = _dot_nt(qj, k)
            m_new = jnp.maximum(m, jnp.max(s, axis=-1, keepdims=True))
            alpha = jnp.exp(m - m_new)
            p = jnp.exp(s - m_new)
            l = alpha * l + jnp.sum(p, axis=-1, keepdims=True)
            acc = alpha * acc + _dot(p.astype(BF16), v)
            out.append((m_new, l, acc))
        return tuple(out)

    one = (jnp.full((tq, 1), -jnp.inf, F32), jnp.zeros((tq, 1), F32), jnp.zeros((tq, A_VDIM), F32))
    carry = (one, one)
    if n_chunks > 0:
        carry = lax.fori_loop(0, n_chunks, lambda c, cr: chunk(pl.multiple_of(c * ck, ck), ck, cr), carry)
    carry = chunk(tail_lo, tail_len, carry)

    lam = lam_ref[...]
    s1 = jnp.sum(lam[0:1, :] * lam[1:2, :], axis=-1, keepdims=True)
    s2 = jnp.sum(lam[2:3, :] * lam[3:4, :], axis=-1, keepdims=True)
    lam_val = jnp.exp(s1) - jnp.exp(s2) + lam_init
    (_, l0, a0), (_, l1, a1) = carry
    o = a0 / l0 - lam_val * (a1 / l1)
    y = o * lax.rsqrt(jnp.mean(o * o, axis=-1, keepdims=True) + RMS_EPS) * g_ref[...] * (1.0 - lam_init)
    o_ref[0] = y.astype(BF16)


def _attention(z, lam_p, subln, *, n_lat, tq, ck, lat_queries, lam_init):
    bsz, n_tok, _ = z.shape
    n_ctx = n_tok - n_lat
    n_q = n_lat if lat_queries else n_ctx
    qb = 0 if lat_queries else n_lat // tq
    kern = functools.partial(_attn_kernel, tq=tq, ck=ck, n_chunks=(n_lat // ck) if lat_queries else 0,
                             tail_lo=n_lat, tail_len=n_ctx, lam_init=lam_init)
    return pl.pallas_call(
        kern,
        grid=(bsz, A_HEADS, n_q // tq),
        in_specs=[
            pl.BlockSpec((8, 128), lambda b, h, i: (0, 0)),
            pl.BlockSpec((1, 128), lambda b, h, i: (0, 0)),
            pl.BlockSpec((1, tq, A_VDIM), lambda b, h, i: (b, qb + i, h)),
            pl.BlockSpec((1, n_tok, A_VDIM), lambda b, h, i: (b, 0, A_HEADS + h)),
            pl.BlockSpec((1, n_tok, A_VDIM), lambda b, h, i: (b, 0, 2 * A_HEADS + h)),
        ],
        out_specs=pl.BlockSpec((1, tq, A_VDIM), lambda b, h, i: (b, i, h)),
        out_shape=jax.ShapeDtypeStruct((bsz, n_q, A_HEADS * A_VDIM), BF16),
        compiler_params=_cparams(3),
        name="attn_lat" if lat_queries else "attn_ctx",
    )(lam_p, subln, z, z, z)


def _mlstm_kernel(qf_ref, kf_ref, vf_ref, gf_ref, gtf_ref, qb_ref, kb_ref, vb_ref, gb_ref, gtb_ref,
                  hf_ref, hb_ref, c_scr, m_scr, *, t):
    step = pl.program_id(1)

    @pl.when(step == 0)
    def _():
        c_scr[...] = jnp.zeros_like(c_scr)
        m_scr[...] = jnp.zeros_like(m_scr)

    ti = lax.broadcasted_iota(jnp.int32, (t, t), 0)
    si = lax.broadcasted_iota(jnp.int32, (t, t), 1)
    lane = lax.broadcasted_iota(jnp.int32, (t, 128), 1)
    ones_blk = jnp.where(lane == 0, 1.0, 0.0).astype(BF16)
    dirs = ((qf_ref, kf_ref, vf_ref, gf_ref, gtf_ref, hf_ref), (qb_ref, kb_ref, vb_ref, gb_ref, gtb_ref, hb_ref))
    for d, (q_ref, k_ref, v_ref, g_ref, gt_ref, h_ref) in enumerate(dirs):
        seen = (si <= ti) if d == 0 else (si >= ti)
        seen_t = (ti <= si) if d == 0 else (ti >= si)
        for hd in range(M_HEADS):
            idx = d * M_HEADS + hd
            q = q_ref[0, :, hd * M_QK:(hd + 1) * M_QK]
            k = k_ref[0, :, hd * M_QK:(hd + 1) * M_QK]
            v = v_ref[0, :, hd * M_V:(hd + 1) * M_V]
            li = d * 2 * M_HEADS + hd
            lf = li + M_HEADS
            i_col = g_ref[0, :, li:li + 1]
            i_row = gt_ref[0, li:li + 1, :]
            lf_col = _log_sigmoid(g_ref[0, :, lf:lf + 1])
            lf_row = _log_sigmoid(gt_ref[0, lf:lf + 1, :])
            b_col = jnp.sum(jnp.where(seen, lf_row, 0.0), axis=1, keepdims=True)
            b_row = jnp.sum(jnp.where(seen_t, lf_col, 0.0), axis=0, keepdims=True)
            b_end = jnp.sum(lf_row, axis=1, keepdims=True)
            m_old = m_scr[idx, 0:1, 0:1]
            dmat = jnp.where(seen, b_col - b_row + i_row, -jnp.inf)
            m_t = jnp.maximum(b_col + m_old, jnp.max(dmat, axis=1, keepdims=True))
            dexp = jnp.exp(dmat - m_t)
            iscale = jnp.exp(b_col + m_old - m_t)
            sc = (_dot_nt(q, k) * dexp).astype(BF16)
            c_aug = c_scr[idx]
            v_aug = jnp.concatenate([v, ones_blk], axis=1)
            inter = _dot(q, c_aug.astype(BF16))
            intra = _dot(sc, v_aug)
            num = iscale * inter[:, :M_V] + intra[:, :M_V]
            den = iscale * inter[:, M_V:M_V + 1] + intra[:, M_V:M_V + 1]
            h_ref[0, :, hd * M_V:(hd + 1) * M_V] = num / jnp.maximum(jnp.abs(den), jnp.exp(-m_t))
            g_col = b_end - b_col + i_col
            m_new = jnp.maximum(b_end + m_old, jnp.max(g_col, axis=0, keepdims=True))
            w_col = jnp.exp(g_col - m_new)
            decay = jnp.exp(b_end + m_old - m_new)
            wv = (w_col * v_aug.astype(F32)).astype(BF16)
            k_t = k.astype(F32).T.astype(BF16)
            c_scr[idx] = decay * c_aug + _dot(k_t, wv)
            m_scr[idx] = jnp.broadcast_to(m_new, (8, 128))


def _mlstm(z, gates, gates_t):
    bsz, n_tok, _ = z.shape
    t = MLSTM_CHUNK
    nc = n_tok // t
    qcol = 3 * COL_TILE // (M_HEADS * M_QK)

    def cf(s):
        return jnp.where(s == 0, nc - 1, s - 1)

    def cb(s):
        return nc - 1 - s

    def specs(c):
        return [
            pl.BlockSpec((1, t, M_HEADS * M_QK), lambda b, s: (b, c(s), qcol)),
            pl.BlockSpec((1, t, M_HEADS * M_QK), lambda b, s: (b, c(s), qcol + 1)),
            pl.BlockSpec((1, t, M_HEADS * M_V), lambda b, s: (b, c(s), 4)),
            pl.BlockSpec((1, t, 128), lambda b, s: (b, c(s), 0)),
            pl.BlockSpec((1, 16, t), lambda b, s: (b, 0, c(s))),
        ]

    out_sd = jax.ShapeDtypeStruct((bsz, n_tok, M_HEADS * M_V), F32)
    return pl.pallas_call(
        functools.partial(_mlstm_kernel, t=t),
        grid=(bsz, nc),
        in_specs=specs(cf) + specs(cb),
        out_specs=[pl.BlockSpec((1, t, M_HEADS * M_V), lambda b, s: (b, cf(s), 0)),
                   pl.BlockSpec((1, t, M_HEADS * M_V), lambda b, s: (b, cb(s), 0))],
        out_shape=[out_sd, out_sd],
        scratch_shapes=[pltpu.VMEM((2 * M_HEADS, M_QK, M_VAUG), F32), pltpu.VMEM((2 * M_HEADS, 8, 128), F32)],
        compiler_params=_cparams(2),
        name="mlstm",
    )(z, z, z, gates, gates_t, z, z, z, gates, gates_t)


def _merge_kernel(x_ref, oal_ref, oac_ref, hf_ref, hb_ref, om_ref, ga_ref, gb_ref, ml_ref, mc_ref, nm_ref,
                  wa_ref, wm_ref, wo_ref, ln_ref, x1_ref, h2_ref, *, tm, n_lat, alpha):
    i = pl.program_id(1)
    rows = i * tm + lax.broadcasted_iota(jnp.int32, (tm, 1), 0)
    is_ctx = rows >= n_lat
    oa = jnp.where(i * tm >= n_lat, oac_ref[0], oal_ref[0])
    hm = hf_ref[0] + hb_ref[0]
    parts = []
    for hd in range(M_HEADS):
        seg = hm[:, hd * M_V:(hd + 1) * M_V]
        parts.append(seg * lax.rsqrt(jnp.mean(seg * seg, axis=-1, keepdims=True) + RMS_EPS))
    hm = jnp.concatenate(parts, axis=1) * nm_ref[...] * _sigmoid(om_ref[0].astype(F32))
    ya = _dot(oa, wa_ref[...])
    ym = _dot(hm.astype(BF16), wm_ref[...])
    y = _sigmoid(ga_ref[0].astype(F32)) * ya + _sigmoid(gb_ref[0].astype(F32)) * ym
    y = _dot(y.astype(BF16), wo_ref[...])
    u = alpha * x_ref[0] + _pick_mod(is_ctx, mc_ref, ml_ref, 2) * y
    x1 = _layer_norm(u, ln_ref[0:1, :], ln_ref[1:2, :])
    x1_ref[0] = x1
    h2_ref[0] = x1 * (1.0 + _pick_mod(is_ctx, mc_ref, ml_ref, 4)) + _pick_mod(is_ctx, mc_ref, ml_ref, 3)


def _merge(xc, oa_lat, oa_ctx, hf, hb, z, mod_lat, mod_ctx, norm_m, w_a, w_m, w_o, ln1, *, n_lat, tm, alpha):
    bsz, n_tok, d = xc.shape
    row = lambda b, i: (b, i, 0)
    const = lambda b, i: (0, 0)
    out_sd = jax.ShapeDtypeStruct((bsz, n_tok, d), F32)
    lat_tiles = n_lat // tm
    return pl.pallas_call(
        functools.partial(_merge_kernel, tm=tm, n_lat=n_lat, alpha=alpha),
        grid=(bsz, n_tok // tm),
        in_specs=[
            pl.BlockSpec((1, tm, d), row),
            pl.BlockSpec((1, tm, d), lambda b, i: (b, jnp.minimum(i, lat_tiles - 1), 0)),
            pl.BlockSpec((1, tm, d), lambda b, i: (b, jnp.maximum(i - lat_tiles, 0), 0)),
            pl.BlockSpec((1, tm, d), row),
            pl.BlockSpec((1, tm, d), row),
            pl.BlockSpec((1, tm, COL_TILE), lambda b, i: (b, i, 5)),
            pl.BlockSpec((1, tm, COL_TILE), lambda b, i: (b, i, 6)),
            pl.BlockSpec((1, tm, COL_TILE), lambda b, i: (b, i, 7)),
            pl.BlockSpec((1, 8, d), lambda b, i: (b, 0, 0)),
            pl.BlockSpec((8, d), const),
            pl.BlockSpec((1, d), const),
            pl.BlockSpec((d, d), const),
            pl.BlockSpec((d, d), const),
            pl.BlockSpec((d, d), const),
            pl.BlockSpec((8, d), const),
        ],
        out_specs=[pl.BlockSpec((1, tm, d), row), pl.BlockSpec((1, tm, d), row)],
        out_shape=[out_sd, out_sd],
        compiler_params=_cparams(2),
        name="merge",
    )(xc, oa_lat, oa_ctx, hf, hb, z, z, z, mod_lat, mod_ctx, norm_m, w_a, w_m, w_o, ln1)


def _router_kernel(h_ref, wr_ref, eb_ref, idx_ref, wn_ref, pos_ref, cnt_ref, carry_scr, *, tm):
    @pl.when(pl.program_id(0) == 0)
    def _():
        carry_scr[...] = jnp.zeros_like(carry_scr)

    scores = _sigmoid(_dot3(h_ref[...], wr_ref[...]))
    biased = scores + eb_ref[...]
    lane = lax.broadcasted_iota(jnp.int32, (tm, N_EXPERTS), 1).astype(F32)
    hots, sels, ws = [], [], []
    for _ in range(TOP_K):
        mx = jnp.max(biased, axis=-1, keepdims=True)
        sel = jnp.min(jnp.where(biased == mx, lane, float(N_EXPERTS)), axis=-1, keepdims=True)
        hot = lane == sel
        hots.append(hot)
        sels.append(sel)
        ws.append(jnp.sum(jnp.where(hot, scores, 0.0), axis=-1, keepdims=True))
        biased = jnp.where(hot, -jnp.inf, biased)
    wsum = ws[0]
    chosen = hots[0]
    for r in range(1, TOP_K):
        wsum = wsum + ws[r]
        chosen = chosen | hots[r]
    chosen_f = jnp.where(chosen, 1.0, 0.0)
    ti = lax.broadcasted_iota(jnp.int32, (tm, tm), 0)
    si = lax.broadcasted_iota(jnp.int32, (tm, tm), 1)
    before = jnp.where(si < ti, 1.0, 0.0).astype(BF16)
    rank = carry_scr[0:1, :] + _dot(before, chosen_f.astype(BF16))
    out_lane = lax.broadcasted_iota(jnp.int32, (tm, 128), 1)
    idx_o = jnp.zeros((tm, 128), jnp.int32)
    pos_o = jnp.zeros((tm, 128), jnp.int32)
    wn_o = jnp.zeros((tm, 128), F32)
    for r in range(TOP_K):
        pos_r = jnp.sum(jnp.where(hots[r], rank, 0.0), axis=-1, keepdims=True).astype(jnp.int32)
        idx_o = jnp.where(out_lane == r, sels[r].astype(jnp.int32), idx_o)
        pos_o = jnp.where(out_lane == r, pos_r, pos_o)
        wn_o = jnp.where(out_lane == r, ws[r] / wsum * ROUTE_SCALE, wn_o)
    idx_ref[...] = idx_o
    pos_ref[...] = pos_o
    wn_ref[...] = wn_o
    total = carry_scr[0:1, :] + jnp.sum(chosen_f, axis=0, keepdims=True)
    carry_scr[...] = jnp.broadcast_to(total, carry_scr.shape)
    cnt_ref[...] = jnp.broadcast_to(total, cnt_ref.shape).astype(jnp.int32)


def _router(h2, w_router, e_bias):
    n, d = h2.shape
    tm = TOK_TILE
    row = lambda i: (i, 0)
    const = lambda i: (0, 0)
    return pl.pallas_call(
        functools.partial(_router_kernel, tm=tm),
        grid=(n // tm,),
        in_specs=[pl.BlockSpec((tm, d), row), pl.BlockSpec((d, N_EXPERTS), const),
                  pl.BlockSpec((1, N_EXPERTS), const)],
        out_specs=[pl.BlockSpec((tm, 128), row), pl.BlockSpec((tm, 128), row), pl.BlockSpec((tm, 128), row),
                   pl.BlockSpec((8, N_EXPERTS), const)],
        out_shape=[jax.ShapeDtypeStruct((n, 128), jnp.int32), jax.ShapeDtypeStruct((n, 128), F32),
                   jax.ShapeDtypeStruct((n, 128), jnp.int32), jax.ShapeDtypeStruct((8, N_EXPERTS), jnp.int32)],
        scratch_shapes=[pltpu.VMEM((8, N_EXPERTS), F32)],
        compiler_params=_cparams(1),
        name="router",
    )(h2, w_router, e_bias)


def _dispatch_kernel(dest_ref, h_ref, xs_in_ref, xs_ref, sem, *, tm):
    del xs_in_ref

    def row_copy(r, k):
        return pltpu.make_async_copy(h_ref.at[pl.ds(r, 1), :],
                                     xs_ref.at[pl.ds(dest_ref[r * TOP_K + k], 1), :], sem)

    def start(r, c):
        for k in range(TOP_K):
            row_copy(r, k).start()
        return c

    def wait(r, c):
        for k in range(TOP_K):
            row_copy(r, k).wait()
        return c

    lax.fori_loop(0, tm, start, 0)
    lax.fori_loop(0, tm, wait, 0)


def _dispatch(dest, h2, xs_init):
    n, d = h2.shape
    tm = TOK_TILE
    return pl.pallas_call(
        functools.partial(_dispatch_kernel, tm=tm),
        grid=(n // tm,),
        in_specs=[pl.BlockSpec((tm * TOP_K,), lambda i: (i,), memory_space=pltpu.SMEM),
                  pl.BlockSpec((tm, d), lambda i: (i, 0)),
                  pl.BlockSpec(memory_space=pl.ANY)],
        out_specs=pl.BlockSpec(memory_space=pl.ANY),
        out_shape=jax.ShapeDtypeStruct(xs_init.shape, xs_init.dtype),
        scratch_shapes=[pltpu.SemaphoreType.DMA(())],
        input_output_aliases={2: 0},
        compiler_params=_cparams(1),
        name="dispatch",
    )(dest, h2, xs_init)


def _expert_kernel(blk_e_ref, n_used_ref, xs_ref, wi_ref, wo_ref, y_ref):
    del blk_e_ref
    used = pl.program_id(0) < n_used_ref[0]

    @pl.when(used)
    def _():
        ag = _dot(xs_ref[...].astype(BF16), wi_ref[0].astype(BF16))
        a = ag[:, :E_HIDDEN]
        g = ag[:, E_HIDDEN:]
        y_ref[...] = _dot((a * _sigmoid(a) * g).astype(BF16), wo_ref[0].astype(BF16))

    @pl.when(jnp.logical_not(used))
    def _():
        y_ref[...] = jnp.zeros_like(y_ref)


def _experts(blk_e, n_used, xs, w_e_in, w_e_out):
    n_slots, d = xs.shape
    n_blk = n_slots // MOE_ROWS

    def blk(i, be, nu):
        return (jnp.minimum(i, nu[0] - 1), 0)

    def wsel(i, be, nu):
        return (be[jnp.minimum(i, nu[0] - 1)], 0, 0)

    return pl.pallas_call(
        _expert_kernel,
        grid_spec=pltpu.PrefetchScalarGridSpec(
            num_scalar_prefetch=2,
            grid=(n_blk,),
            in_specs=[pl.BlockSpec((MOE_ROWS, d), blk),
                      pl.BlockSpec((1, d, 2 * E_HIDDEN), wsel),
                      pl.BlockSpec((1, E_HIDDEN, d), wsel)],
            out_specs=pl.BlockSpec((MOE_ROWS, d), lambda i, be, nu: (i, 0)),
        ),
        out_shape=jax.ShapeDtypeStruct((n_slots, d), F32),
        compiler_params=_cparams(1),
        name="experts",
    )(blk_e, n_used, xs, w_e_in, w_e_out)


def _combine_kernel(dest_ref, y_ref, h_ref, x1_ref, wn_ref, ml_ref, mc_ref, wsi_ref, wso_ref, ln_ref,
                    x2_ref, g_scr, sem, *, tm, n_lat, n_tok, alpha):
    def row_copy(r, k):
        return pltpu.make_async_copy(y_ref.at[pl.ds(dest_ref[r * TOP_K + k], 1), :],
                                     g_scr.at[k, pl.ds(r, 1), :], sem)

    def start(r, c):
        for k in range(TOP_K):
            row_copy(r, k).start()
        return c

    def wait(r, c):
        for k in range(TOP_K):
            row_copy(r, k).wait()
        return c

    lax.fori_loop(0, tm, start, 0)
    h = h_ref[...]
    ag = _dot(h.astype(BF16), wsi_ref[...])
    a = ag[:, :S_HIDDEN]
    g = ag[:, S_HIDDEN:]
    f = _dot((a * _sigmoid(a) * g).astype(BF16), wso_ref[...])
    lax.fori_loop(0, tm, wait, 0)
    wn = wn_ref[...]
    for k in range(TOP_K):
        f = f + wn[:, k:k + 1] * g_scr[k]
    rows = (pl.program_id(0) * tm) % n_tok + lax.broadcasted_iota(jnp.int32, (tm, 1), 0)
    is_ctx = rows >= n_lat
    u = alpha * x1_ref[...] + _pick_mod(is_ctx, mc_ref, ml_ref, 5) * f
    x2_ref[...] = _layer_norm(u, ln_ref[0:1, :], ln_ref[1:2, :])


def _combine(dest, y, h2, x1, wn, mod_lat, mod_ctx, ws_in, ws_out, ln2, *, n_lat, n_tok, alpha):
    n, d = h2.shape
    tm = TOK_TILE
    tiles_per_batch = n_tok // tm
    row = lambda i: (i, 0)
    const = lambda i: (0, 0)
    return pl.pallas_call(
        functools.partial(_combine_kernel, tm=tm, n_lat=n_lat, n_tok=n_tok, alpha=alpha),
        grid=(n // tm,),
        in_specs=[pl.BlockSpec((tm * TOP_K,), lambda i: (i,), memory_space=pltpu.SMEM),
                  pl.BlockSpec(memory_space=pl.ANY),
                  pl.BlockSpec((tm, d), row),
                  pl.BlockSpec((tm, d), row),
                  pl.BlockSpec((tm, 128), row),
                  pl.BlockSpec((1, 8, d), lambda i: (i // tiles_per_batch, 0, 0)),
                  pl.BlockSpec((8, d), const),
                  pl.BlockSpec((d, 2 * S_HIDDEN), const),
                  pl.BlockSpec((S_HIDDEN, d), const),
                  pl.BlockSpec((8, d), const)],
        out_specs=pl.BlockSpec((tm, d), row),
        out_shape=jax.ShapeDtypeStruct((n, d), F32),
        scratch_shapes=[pltpu.VMEM((TOP_K, tm, d), F32), pltpu.SemaphoreType.DMA(())],
        compiler_params=_cparams(1),
        name="combine",
    )(dest, y, h2, x1, wn, mod_lat, mod_ctx, ws_in, ws_out, ln2)


def _rope_tables(n_lat, n_ctx):
    rows = n_lat // GRID_W
    row = jnp.repeat(jnp.arange(rows, dtype=F32), GRID_W)
    col = jnp.tile(jnp.arange(GRID_W, dtype=F32), rows)
    n_freq = A_DIM // 4
    inv = jnp.power(ROPE_THETA, -jnp.arange(n_freq, dtype=F32) / n_freq)
    ang_r = row[:, None] * inv
    ang_c = col[:, None] * inv
    cos = jnp.concatenate([jnp.cos(ang_r), jnp.cos(ang_r), jnp.cos(ang_c), jnp.cos(ang_c)], axis=1)
    sin = jnp.concatenate([-jnp.sin(ang_r), jnp.sin(ang_r), -jnp.sin(ang_c), jnp.sin(ang_c)], axis=1)
    cos = jnp.concatenate([jnp.tile(cos, (1, 2)), jnp.ones((n_ctx, 128), F32)], axis=0)
    sin = jnp.concatenate([jnp.tile(sin, (1, 2)), jnp.zeros((n_ctx, 128), F32)], axis=0)
    return cos, sin


def _row_tile(n_tok):
    for tm in (1280, 640, 256):
        if n_tok % tm == 0:
            return tm
    raise ValueError(f"unsupported token count {n_tok}")


def _pad_rows(a, rows):
    return jnp.concatenate([a, jnp.zeros((rows - a.shape[0],) + a.shape[1:], a.dtype)], axis=0)


def kernel(x, c, ctx, c_ctx, ada_w, ada_b, w_in, b_gates, conv_qk, lam, subln_a, norm_m, w_br_a, w_br_m, w_out,
           ln1_g, ln1_b, ln2_g, ln2_b, w_router, e_bias, w_e_in, w_e_out, ws_in, ws_out):
    bsz, n_lat, d = x.shape
    n_ctx = ctx.shape[1]
    n_tok = n_lat + n_ctx
    depth = ada_w.shape[0]
    assert d == D_MODEL and n_ctx == MLSTM_CHUNK and n_lat % 512 == 0 and bsz + 1 <= 8
    alpha = (2 * depth) ** 0.25
    tm = _row_tile(n_tok)
    tq = 512
    n_all = bsz * n_tok
    n_assign = n_all * TOP_K
    n_blk = -(-n_assign // MOE_ROWS) + N_EXPERTS
    n_slots = n_blk * MOE_ROWS

    cond = _pad_rows(jnp.concatenate([c, c_ctx[None, :]], axis=0), 8)
    mods = _ada(cond, ada_w, ada_b).reshape(depth, 8, N_MOD, d)
    cos_t, sin_t = _rope_tables(n_lat, n_ctx)
    xc = jnp.concatenate([x, ctx], axis=1)
    g_lo = 6 * COL_TILE
    g_hi = g_lo + 4 * M_HEADS

    for l in range(depth):
        lam_init = 0.8 - 0.6 * math.exp(-0.3 * l)
        mod_lat = _pad_rows(mods[l, :bsz].transpose(1, 0, 2), 8).transpose(1, 0, 2)
        mod_ctx = _pad_rows(mods[l, bsz], 8)
        w_main = jnp.concatenate([w_in[l, :, :g_lo], w_in[l, :, g_hi:]], axis=1).astype(BF16)
        w_gates = jnp.pad(w_in[l, :, g_lo:g_hi], ((0, 0), (0, 128 - 4 * M_HEADS)))
        bg = jnp.pad(b_gates[l], (0, 128 - 4 * M_HEADS))[None, :]
        conv_w = _pad_rows(conv_qk[l], 8)
        lam_p = jnp.pad(lam[l], ((0, 4), (0, 128 - A_DIM)))
        subln = subln_a[l][None, :]

        z, gates = _proj(xc, mod_lat, mod_ctx, w_main, w_gates, bg, cos_t, sin_t, conv_w, n_lat=n_lat, tm=tm)
        oa_lat = _attention(z, lam_p, subln, n_lat=n_lat, tq=tq, ck=512, lat_queries=True, lam_init=lam_init)
        oa_ctx = _attention(z, lam_p, subln, n_lat=n_lat, tq=n_ctx, ck=512, lat_queries=False, lam_init=lam_init)
        gates_t = gates[:, :, :4 * M_HEADS].transpose(0, 2, 1)
        hf, hb = _mlstm(z, gates, gates_t)
        ln1 = _pad_rows(jnp.stack([ln1_g[l], ln1_b[l]]), 8)
        x1, h2 = _merge(xc, oa_lat, oa_ctx, hf, hb, z, mod_lat, mod_ctx, norm_m[l][None, :], w_br_a[l].astype(BF16),
                        w_br_m[l].astype(BF16), w_out[l].astype(BF16), ln1, n_lat=n_lat, tm=TOK_TILE, alpha=alpha)

        h2f = h2.reshape(n_all, d)
        idx, wn, pos, cnt = _router(h2f, w_router[l], e_bias[l][None, :])
        counts = cnt[0]
        padded = (counts + MOE_ROWS - 1) // MOE_ROWS * MOE_ROWS
        pend = jnp.cumsum(padded)
        pstart = pend - padded
        dest = (pstart[idx[:, :TOP_K]] + pos[:, :TOP_K]).reshape(-1).astype(jnp.int32)
        blk_e = jnp.minimum(jnp.searchsorted(pend, jnp.arange(n_blk, dtype=jnp.int32) * MOE_ROWS, side="right"),
                            N_EXPERTS - 1).astype(jnp.int32)
        n_used = (pend[-1:] // MOE_ROWS).astype(jnp.int32)
        xs = _dispatch(dest, h2f, jnp.zeros((n_slots, d), F32))
        y = _experts(blk_e, n_used, xs, w_e_in[l], w_e_out[l])
        ln2 = _pad_rows(jnp.stack([ln2_g[l], ln2_b[l]]), 8)
        x2 = _combine(dest, y, h2f, x1.reshape(n_all, d), wn, mod_lat, mod_ctx, ws_in[l].astype(BF16),
                      ws_out[l].astype(BF16), ln2, n_lat=n_lat, n_tok=n_tok, alpha=alpha)
        xc = x2.reshape(bsz, n_tok, d)
    return xc[:, :n_lat]
```

```python
import functools
import math

import jax
import jax.numpy as jnp
from jax import lax
from jax.experimental import pallas as pl
from jax.experimental.pallas import tpu as pltpu

F32 = jnp.float32
BF16 = jnp.bfloat16

D_MODEL = 1024
GRID_W = 64
A_HEADS = 8
A_DIM = 64
A_VDIM = 2 * A_DIM
ROPE_THETA = 10000.0
Q_SCALE = A_DIM ** -0.5 * math.log2(math.e)
M_HEADS = 4
M_QK = 128
M_V = 256
M_VAUG = M_V + 128
N_EXPERTS = 256
TOP_K = 8
E_HIDDEN = 256
S_HIDDEN = 256
ROUTE_SCALE = 2.5
LN_EPS = 1e-5
RMS_EPS = 1e-6
N_MOD = 6
COL_TILE = 1024
N_COL_TILES = 8
ATTN_UNROLL = 4
MLSTM_CHUNK = 256
MOE_ROWS = 256
TOK_TILE = 256
VMEM_LIMIT = 56 * 1024 * 1024


def _cparams(n_axes):
    return pltpu.CompilerParams(dimension_semantics=("arbitrary",) * n_axes,
                                vmem_limit_bytes=VMEM_LIMIT)


def _dot(a, b):
    return jnp.dot(a, b, preferred_element_type=F32)


def _dot_nt(a, b):
    return lax.dot_general(a, b, (((1,), (1,)), ((), ())), preferred_element_type=F32)


def _split_bf16(a):
    hi = a.astype(BF16)
    lo = (a - hi.astype(F32)).astype(BF16)
    return hi, lo


def _dot3(a, b):
    ah, al = _split_bf16(a)
    bh, bl = _split_bf16(b)
    return _dot(ah, bh) + _dot(al, bh) + _dot(ah, bl)


def _sigmoid(v):
    return 1.0 / (1.0 + jnp.exp(-v))


def _log_sigmoid(v):
    return jnp.minimum(v, 0.0) - jnp.log(1.0 + jnp.exp(-jnp.abs(v)))


def _layer_norm(u, g, b):
    mu = jnp.mean(u, axis=-1, keepdims=True)
    var = jnp.mean(jnp.square(u - mu), axis=-1, keepdims=True)
    return (u - mu) * lax.rsqrt(var + LN_EPS) * g + b


def _pick_mod(is_ctx, mc_ref, ml_ref, k):
    return jnp.where(is_ctx, mc_ref[k:k + 1, :], ml_ref[0, k:k + 1, :])


def _ada_kernel(cond_ref, w_ref, b_ref, o_ref):
    c = cond_ref[...]
    c = c * _sigmoid(c)
    o_ref[0] = _dot3(c, w_ref[0]) + b_ref[0]


def _ada(cond, ada_w, ada_b):
    depth, d, width = ada_w.shape
    tn = 512
    return pl.pallas_call(
        _ada_kernel,
        grid=(depth, width // tn),
        in_specs=[pl.BlockSpec((8, d), lambda l, j: (0, 0)),
                  pl.BlockSpec((1, d, tn), lambda l, j: (l, 0, j)),
                  pl.BlockSpec((1, 1, tn), lambda l, j: (l, 0, j))],
        out_specs=pl.BlockSpec((1, 8, tn), lambda l, j: (l, 0, j)),
        out_shape=jax.ShapeDtypeStruct((depth, 8, width), F32),
        compiler_params=_cparams(2),
        name="ada",
    )(cond, ada_w, ada_b.reshape(depth, 1, width))


def _proj_kernel(x_ref, xp_ref, xn_ref, ml_ref, mc_ref, w_ref, wg_ref, bg_ref, cos_ref, sin_ref,
                 conv_ref, z_ref, gates_ref, h_scr, halo_scr, *, tm, n_lat, n_tok):
    i = pl.program_id(1)
    j = pl.program_id(2)
    row0 = i * tm

    def modulate(xv, rows):
        is_ctx = rows >= n_lat
        return xv * (1.0 + _pick_mod(is_ctx, mc_ref, ml_ref, 1)) + _pick_mod(is_ctx, mc_ref, ml_ref, 0)

    @pl.when(j == 0)
    def _():
        rows = row0 + lax.broadcasted_iota(jnp.int32, (tm, 1), 0)
        h = modulate(x_ref[0], rows)
        h_scr[...] = h.astype(BF16)
        r8 = lax.broadcasted_iota(jnp.int32, (8, 1), 0)
        halo_scr[0:8, :] = modulate(xp_ref[0], row0 - 8 + r8)
        halo_scr[8:16, :] = modulate(xn_ref[0], row0 + tm + r8)
        gates_ref[0] = _dot3(h, wg_ref[...]) + bg_ref[...]

    zt = _dot(h_scr[...], w_ref[...])

    @pl.when(j <= 1)
    def _():
        lane = lax.broadcasted_iota(jnp.int32, (tm, 128), 1)
        first = (lane % 32) < 16
        scale = jnp.where(j == 0, Q_SCALE, 1.0).astype(F32)
        cos = cos_ref[...] * scale
        sin = sin_ref[...] * scale
        for g in range(COL_TILE // 128):
            zg = zt[:, g * 128:(g + 1) * 128]
            partner = jnp.where(first, pltpu.roll(zg, 128 - 16, 1), pltpu.roll(zg, 16, 1))
            z_ref[0, :, g * 128:(g + 1) * 128] = (zg * cos + partner * sin).astype(BF16)

    @pl.when(j == 3)
    def _():
        zh = _dot(halo_scr[...].astype(BF16), w_ref[...])
        loc = lax.broadcasted_iota(jnp.int32, (tm, 1), 0)
        rows = row0 + loc
        seg_start = (rows == 0) | (rows == n_lat)
        seg_end = (rows == n_lat - 1) | (rows == n_tok - 1)
        zd = jnp.where(loc == 0, zh[7:8, :], pltpu.roll(zt, 1, 0))
        zd = jnp.where(seg_start, 0.0, zd)
        zu = jnp.where(loc == tm - 1, zh[8:9, :], pltpu.roll(zt, tm - 1, 0))
        zu = jnp.where(seg_end, 0.0, zu)
        y = zd * conv_ref[0:1, :] + zt * conv_ref[1:2, :] + zu * conv_ref[2:3, :]
        y = y * _sigmoid(y)
        lane = lax.broadcasted_iota(jnp.int32, (1, COL_TILE), 1)
        y = y * jnp.where(lane >= M_HEADS * M_QK, M_QK ** -0.5, 1.0).astype(F32)
        z_ref[0] = y.astype(BF16)

    @pl.when((j == 2) | (j >= 4))
    def _():
        z_ref[0] = zt.astype(BF16)


def _proj(xc, mod_lat, mod_ctx, w_main, w_gates, b_gates, cos_t, sin_t, conv_w, *, n_lat, tm):
    bsz, n_tok, d = xc.shape
    nt8 = n_tok // 8
    kern = functools.partial(_proj_kernel, tm=tm, n_lat=n_lat, n_tok=n_tok)
    return pl.pallas_call(
        kern,
        grid=(bsz, n_tok // tm, N_COL_TILES),
        in_specs=[
            pl.BlockSpec((1, tm, d), lambda b, i, j: (b, i, 0)),
            pl.BlockSpec((1, 8, d), lambda b, i, j: (b, jnp.maximum(i * (tm // 8) - 1, 0), 0)),
            pl.BlockSpec((1, 8, d), lambda b, i, j: (b, jnp.minimum((i + 1) * (tm // 8), nt8 - 1), 0)),
            pl.BlockSpec((1, 8, d), lambda b, i, j: (b, 0, 0)),
            pl.BlockSpec((8, d), lambda b, i, j: (0, 0)),
            pl.BlockSpec((d, COL_TILE), lambda b, i, j: (0, j)),
            pl.BlockSpec((d, 128), lambda b, i, j: (0, 0)),
            pl.BlockSpec((1, 128), lambda b, i, j: (0, 0)),
            pl.BlockSpec((tm, 128), lambda b, i, j: (i, 0)),
            pl.BlockSpec((tm, 128), lambda b, i, j: (i, 0)),
            pl.BlockSpec((8, COL_TILE), lambda b, i, j: (0, 0)),
        ],
        out_specs=[
            pl.BlockSpec((1, tm, COL_TILE), lambda b, i, j: (b, i, j)),
            pl.BlockSpec((1, tm, 128), lambda b, i, j: (b, i, 0)),
        ],
        out_shape=[jax.ShapeDtypeStruct((bsz, n_tok, N_COL_TILES * COL_TILE), BF16),
                   jax.ShapeDtypeStruct((bsz, n_tok, 128), F32)],
        scratch_shapes=[pltpu.VMEM((tm, d), BF16), pltpu.VMEM((16, d), F32)],
        compiler_params=_cparams(3),
        name="proj",
    )(xc, xc, xc, mod_lat, mod_ctx, w_main, w_gates, b_gates, cos_t, sin_t, conv_w)


def _attn_kernel(lam_ref, g_ref, q_ref, k_ref, v_ref, o_ref, *s_scr, tq, ck, n_chunks, kv_lo, lam_init):
    q = q_ref[0]
    lane = lax.broadcasted_iota(jnp.int32, (1, A_VDIM), 1)
    zero = jnp.zeros_like(q)
    qs = (jnp.where(lane < A_DIM, q, zero), jnp.where(lane >= A_DIM, q, zero))
    slots = (s_scr[0:2], s_scr[2:4])

    ones_blk = jnp.where(lax.broadcasted_iota(jnp.int32, (ck, A_VDIM), 1) == 0, 1.0, 0.0).astype(BF16)

    def kv_start(c):
        return pl.multiple_of(kv_lo + c * ck, 128)

    def scores(c, slot):
        k = k_ref[0, pl.ds(kv_start(c), ck), :]
        for qj, s_ref in zip(qs, slots[slot]):
            s_ref[...] = _dot_nt(qj, k)

    def absorb(c, slot, carry):
        v_aug = jnp.concatenate([v_ref[0, pl.ds(kv_start(c), ck), :], ones_blk], axis=1)
        out = []
        for s_ref, (m, acc) in zip(slots[slot], carry):
            s = s_ref[...]
            m_new = jnp.maximum(m, jnp.max(s, axis=-1, keepdims=True))
            alpha = jnp.exp2(m - m_new)
            p = jnp.exp2(s - m_new)
            acc = alpha * acc + _dot(p.astype(BF16), v_aug)
            out.append((m_new, acc))
        return tuple(out)

    def run(c0, count, carry):
        for u in range(count):
            scores(c0 + u + 1, (u + 1) % 2)
            carry = absorb(c0 + u, u % 2, carry)
        return carry

    one = (jnp.full((tq, 1), -jnp.inf, F32), jnp.zeros((tq, 2 * A_VDIM), F32))
    scores(0, 0)
    trips = (n_chunks - 1) // ATTN_UNROLL
    carry = (one, one)
    if trips > 0:
        carry = lax.fori_loop(0, trips, lambda i, cr: run(i * ATTN_UNROLL, ATTN_UNROLL, cr), carry)
    rest = n_chunks - 1 - trips * ATTN_UNROLL
    carry = run(trips * ATTN_UNROLL, rest, carry)
    carry = absorb(n_chunks - 1, rest % 2, carry)

    lam = lam_ref[...]
    s1 = jnp.sum(lam[0:1, :] * lam[1:2, :], axis=-1, keepdims=True)
    s2 = jnp.sum(lam[2:3, :] * lam[3:4, :], axis=-1, keepdims=True)
    lam_val = jnp.exp(s1) - jnp.exp(s2) + lam_init
    (_, acc0), (_, acc1) = carry
    o = (acc0[:, :A_VDIM] / acc0[:, A_VDIM:A_VDIM + 1]
         - lam_val * (acc1[:, :A_VDIM] / acc1[:, A_VDIM:A_VDIM + 1]))
    y = o * lax.rsqrt(jnp.mean(o * o, axis=-1, keepdims=True) + RMS_EPS) * g_ref[...] * (1.0 - lam_init)
    o_ref[0] = y.astype(BF16)


def _attention(z, lam_p, subln, *, n_lat, tq, ck, lat_queries, lam_init):
    bsz, n_tok, _ = z.shape
    n_ctx = n_tok - n_lat
    n_q = n_lat if lat_queries else n_ctx
    qb = 0 if lat_queries else n_lat // tq
    if not lat_queries:
        ck = n_ctx
    kern = functools.partial(_attn_kernel, tq=tq, ck=ck, n_chunks=(n_tok if lat_queries else n_ctx) // ck,
                             kv_lo=0 if lat_queries else n_lat, lam_init=lam_init)
    return pl.pallas_call(
        kern,
        grid=(bsz, A_HEADS, n_q // tq),
        in_specs=[
            pl.BlockSpec((8, 128), lambda b, h, i: (0, 0)),
            pl.BlockSpec((1, 128), lambda b, h, i: (0, 0)),
            pl.BlockSpec((1, tq, A_VDIM), lambda b, h, i: (b, qb + i, h)),
            pl.BlockSpec((1, n_tok, A_VDIM), lambda b, h, i: (b, 0, A_HEADS + h)),
            pl.BlockSpec((1, n_tok, A_VDIM), lambda b, h, i: (b, 0, 2 * A_HEADS + h)),
        ],
        out_specs=pl.BlockSpec((1, tq, A_VDIM), lambda b, h, i: (b, i, h)),
        out_shape=jax.ShapeDtypeStruct((bsz, n_q, A_HEADS * A_VDIM), BF16),
        scratch_shapes=[pltpu.VMEM((tq, ck), F32)] * 4,
        compiler_params=_cparams(3),
        name="attn_lat" if lat_queries else "attn_ctx",
    )(lam_p, subln, z, z, z)


def _mlstm_kernel(qf_ref, kf_ref, vf_ref, gf_ref, gtf_ref, qb_ref, kb_ref, vb_ref, gb_ref, gtb_ref,
                  hf_ref, hb_ref, c_scr, m_scr, *, t):
    step = pl.program_id(1)

    @pl.when(step == 0)
    def _():
        c_scr[...] = jnp.zeros_like(c_scr)
        m_scr[...] = jnp.zeros_like(m_scr)

    ti = lax.broadcasted_iota(jnp.int32, (t, t), 0)
    si = lax.broadcasted_iota(jnp.int32, (t, t), 1)
    lane = lax.broadcasted_iota(jnp.int32, (t, 128), 1)
    ones_blk = jnp.where(lane == 0, 1.0, 0.0).astype(BF16)
    dirs = ((qf_ref, kf_ref, vf_ref, gf_ref, gtf_ref, hf_ref), (qb_ref, kb_ref, vb_ref, gb_ref, gtb_ref, hb_ref))
    for d, (q_ref, k_ref, v_ref, g_ref, gt_ref, h_ref) in enumerate(dirs):
        seen = (si <= ti) if d == 0 else (si >= ti)
        seen_t = (ti <= si) if d == 0 else (ti >= si)
        for hd in range(M_HEADS):
            idx = d * M_HEADS + hd
            q = q_ref[0, :, hd * M_QK:(hd + 1) * M_QK]
            k = k_ref[0, :, hd * M_QK:(hd + 1) * M_QK]
            v = v_ref[0, :, hd * M_V:(hd + 1) * M_V]
            li = d * 2 * M_HEADS + hd
            lf = li + M_HEADS
            i_col = g_ref[0, :, li:li + 1]
            i_row = gt_ref[0, li:li + 1, :]
            lf_col = _log_sigmoid(g_ref[0, :, lf:lf + 1])
            lf_row = _log_sigmoid(gt_ref[0, lf:lf + 1, :])
            b_col = jnp.sum(jnp.where(seen, lf_row, 0.0), axis=1, keepdims=True)
            b_row = jnp.sum(jnp.where(seen_t, lf_col, 0.0), axis=0, keepdims=True)
            b_end = jnp.sum(lf_row, axis=1, keepdims=True)
            m_old = m_scr[idx, 0:1, 0:1]
            dmat = jnp.where(seen, b_col - b_row + i_row, -jnp.inf)
            m_t = jnp.maximum(b_col + m_old, jnp.max(dmat, axis=1, keepdims=True))
            dexp = jnp.exp(dmat - m_t)
            iscale = jnp.exp(b_col + m_old - m_t)
            sc = (_dot_nt(q, k) * dexp).astype(BF16)
            c_aug = c_scr[idx]
            v_aug = jnp.concatenate([v, ones_blk], axis=1)
            inter = _dot(q, c_aug.astype(BF16))
            intra = _dot(sc, v_aug)
            num = iscale * inter[:, :M_V] + intra[:, :M_V]
            den = iscale * inter[:, M_V:M_V + 1] + intra[:, M_V:M_V + 1]
            h_ref[0, :, hd * M_V:(hd + 1) * M_V] = num / jnp.maximum(jnp.abs(den), jnp.exp(-m_t))
            g_col = b_end - b_col + i_col
            m_new = jnp.maximum(b_end + m_old, jnp.max(g_col, axis=0, keepdims=True))
            w_col = jnp.exp(g_col - m_new)
            decay = jnp.exp(b_end + m_old - m_new)
            wv = (w_col * v_aug.astype(F32)).astype(BF16)
            k_t = k.astype(F32).T.astype(BF16)
            c_scr[idx] = decay * c_aug + _dot(k_t, wv)
            m_scr[idx] = jnp.broadcast_to(m_new, (8, 128))


def _mlstm(z, gates, gates_t):
    bsz, n_tok, _ = z.shape
    t = MLSTM_CHUNK
    nc = n_tok // t
    qcol = 3 * COL_TILE // (M_HEADS * M_QK)

    def cf(s):
        return jnp.where(s == 0, nc - 1, s - 1)

    def cb(s):
        return nc - 1 - s

    def specs(c):
        return [
            pl.BlockSpec((1, t, M_HEADS * M_QK), lambda b, s: (b, c(s), qcol)),
            pl.BlockSpec((1, t, M_HEADS * M_QK), lambda b, s: (b, c(s), qcol + 1)),
            pl.BlockSpec((1, t, M_HEADS * M_V), lambda b, s: (b, c(s), 4)),
            pl.BlockSpec((1, t, 128), lambda b, s: (b, c(s), 0)),
            pl.BlockSpec((1, 16, t), lambda b, s: (b, 0, c(s))),
        ]

    out_sd = jax.ShapeDtypeStruct((bsz, n_tok, M_HEADS * M_V), F32)
    return pl.pallas_call(
        functools.partial(_mlstm_kernel, t=t),
        grid=(bsz, nc),
        in_specs=specs(cf) + specs(cb),
        out_specs=[pl.BlockSpec((1, t, M_HEADS * M_V), lambda b, s: (b, cf(s), 0)),
                   pl.BlockSpec((1, t, M_HEADS * M_V), lambda b, s: (b, cb(s), 0))],
        out_shape=[out_sd, out_sd],
        scratch_shapes=[pltpu.VMEM((2 * M_HEADS, M_QK, M_VAUG), F32), pltpu.VMEM((2 * M_HEADS, 8, 128), F32)],
        compiler_params=_cparams(2),
        name="mlstm",
    )(z, z, z, gates, gates_t, z, z, z, gates, gates_t)


def _merge_kernel(x_ref, oal_ref, oac_ref, hf_ref, hb_ref, om_ref, ga_ref, gb_ref, ml_ref, mc_ref, nm_ref,
                  wa_ref, wm_ref, wo_ref, ln_ref, x1_ref, h2_ref, *, tm, n_lat, alpha):
    i = pl.program_id(1)
    rows = i * tm + lax.broadcasted_iota(jnp.int32, (tm, 1), 0)
    is_ctx = rows >= n_lat
    oa = jnp.where(i * tm >= n_lat, oac_ref[0], oal_ref[0])
    hm = hf_ref[0] + hb_ref[0]
    parts = []
    for hd in range(M_HEADS):
        seg = hm[:, hd * M_V:(hd + 1) * M_V]
        parts.append(seg * lax.rsqrt(jnp.mean(seg * seg, axis=-1, keepdims=True) + RMS_EPS))
    hm = jnp.concatenate(parts, axis=1) * nm_ref[...] * _sigmoid(om_ref[0].astype(F32))
    ya = _dot(oa, wa_ref[...])
    ym = _dot(hm.astype(BF16), wm_ref[...])
    y = _sigmoid(ga_ref[0].astype(F32)) * ya + _sigmoid(gb_ref[0].astype(F32)) * ym
    y = _dot(y.astype(BF16), wo_ref[...])
    u = alpha * x_ref[0] + _pick_mod(is_ctx, mc_ref, ml_ref, 2) * y
    x1 = _layer_norm(u, ln_ref[0:1, :], ln_ref[1:2, :])
    x1_ref[0] = x1
    h2_ref[0] = x1 * (1.0 + _pick_mod(is_ctx, mc_ref, ml_ref, 4)) + _pick_mod(is_ctx, mc_ref, ml_ref, 3)


def _merge(xc, oa_lat, oa_ctx, hf, hb, z, mod_lat, mod_ctx, norm_m, w_a, w_m, w_o, ln1, *, n_lat, tm, alpha):
    bsz, n_tok, d = xc.shape
    row = lambda b, i: (b, i, 0)
    const = lambda b, i: (0, 0)
    out_sd = jax.ShapeDtypeStruct((bsz, n_tok, d), F32)
    lat_tiles = n_lat // tm
    return pl.pallas_call(
        functools.partial(_merge_kernel, tm=tm, n_lat=n_lat, alpha=alpha),
        grid=(bsz, n_tok // tm),
        in_specs=[
            pl.BlockSpec((1, tm, d), row),
            pl.BlockSpec((1, tm, d), lambda b, i: (b, jnp.minimum(i, lat_tiles - 1), 0)),
            pl.BlockSpec((1, tm, d), lambda b, i: (b, jnp.maximum(i - lat_tiles, 0), 0)),
            pl.BlockSpec((1, tm, d), row),
            pl.BlockSpec((1, tm, d), row),
            pl.BlockSpec((1, tm, COL_TILE), lambda b, i: (b, i, 5)),
            pl.BlockSpec((1, tm, COL_TILE), lambda b, i: (b, i, 6)),
            pl.BlockSpec((1, tm, COL_TILE), lambda b, i: (b, i, 7)),
            pl.BlockSpec((1, 8, d), lambda b, i: (b, 0, 0)),
            pl.BlockSpec((8, d), const),
            pl.BlockSpec((1, d), const),
            pl.BlockSpec((d, d), const),
            pl.BlockSpec((d, d), const),
            pl.BlockSpec((d, d), const),
            pl.BlockSpec((8, d), const),
        ],
        out_specs=[pl.BlockSpec((1, tm, d), row), pl.BlockSpec((1, tm, d), row)],
        out_shape=[out_sd, out_sd],
        compiler_params=_cparams(2),
        name="merge",
    )(xc, oa_lat, oa_ctx, hf, hb, z, z, z, mod_lat, mod_ctx, norm_m, w_a, w_m, w_o, ln1)


def _router_kernel(h_ref, wr_ref, eb_ref, idx_ref, wn_ref, pos_ref, cnt_ref, carry_scr, *, tm):
    @pl.when(pl.program_id(0) == 0)
    def _():
        carry_scr[...] = jnp.zeros_like(carry_scr)

    scores = _sigmoid(_dot3(h_ref[...], wr_ref[...]))
    biased = scores + eb_ref[...]
    lane = lax.broadcasted_iota(jnp.int32, (tm, N_EXPERTS), 1).astype(F32)
    hots, sels, ws = [], [], []
    for _ in range(TOP_K):
        mx = jnp.max(biased, axis=-1, keepdims=True)
        sel = jnp.min(jnp.where(biased == mx, lane, float(N_EXPERTS)), axis=-1, keepdims=True)
        hot = lane == sel
        hots.append(hot)
        sels.append(sel)
        ws.append(jnp.sum(jnp.where(hot, scores, 0.0), axis=-1, keepdims=True))
        biased = jnp.where(hot, -jnp.inf, biased)
    wsum = ws[0]
    chosen = hots[0]
    for r in range(1, TOP_K):
        wsum = wsum + ws[r]
        chosen = chosen | hots[r]
    chosen_f = jnp.where(chosen, 1.0, 0.0)
    ti = lax.broadcasted_iota(jnp.int32, (tm, tm), 0)
    si = lax.broadcasted_iota(jnp.int32, (tm, tm), 1)
    before = jnp.where(si < ti, 1.0, 0.0).astype(BF16)
    rank = carry_scr[0:1, :] + _dot(before, chosen_f.astype(BF16))
    out_lane = lax.broadcasted_iota(jnp.int32, (tm, 128), 1)
    idx_o = jnp.zeros((tm, 128), jnp.int32)
    pos_o = jnp.zeros((tm, 128), jnp.int32)
    wn_o = jnp.zeros((tm, 128), F32)
    for r in range(TOP_K):
        pos_r = jnp.sum(jnp.where(hots[r], rank, 0.0), axis=-1, keepdims=True).astype(jnp.int32)
        idx_o = jnp.where(out_lane == r, sels[r].astype(jnp.int32), idx_o)
        pos_o = jnp.where(out_lane == r, pos_r, pos_o)
        wn_o = jnp.where(out_lane == r, ws[r] / wsum * ROUTE_SCALE, wn_o)
    idx_ref[...] = idx_o
    pos_ref[...] = pos_o
    wn_ref[...] = wn_o
    total = carry_scr[0:1, :] + jnp.sum(chosen_f, axis=0, keepdims=True)
    carry_scr[...] = jnp.broadcast_to(total, carry_scr.shape)
    cnt_ref[...] = jnp.broadcast_to(total, cnt_ref.shape).astype(jnp.int32)


def _router(h2, w_router, e_bias):
    n, d = h2.shape
    tm = TOK_TILE
    row = lambda i: (i, 0)
    const = lambda i: (0, 0)
    return pl.pallas_call(
        functools.partial(_router_kernel, tm=tm),
        grid=(n // tm,),
        in_specs=[pl.BlockSpec((tm, d), row), pl.BlockSpec((d, N_EXPERTS), const),
                  pl.BlockSpec((1, N_EXPERTS), const)],
        out_specs=[pl.BlockSpec((tm, 128), row), pl.BlockSpec((tm, 128), row), pl.BlockSpec((tm, 128), row),
                   pl.BlockSpec((8, N_EXPERTS), const)],
        out_shape=[jax.ShapeDtypeStruct((n, 128), jnp.int32), jax.ShapeDtypeStruct((n, 128), F32),
                   jax.ShapeDtypeStruct((n, 128), jnp.int32), jax.ShapeDtypeStruct((8, N_EXPERTS), jnp.int32)],
        scratch_shapes=[pltpu.VMEM((8, N_EXPERTS), F32)],
        compiler_params=_cparams(1),
        name="router",
    )(h2, w_router, e_bias)


def _slots_kernel(idx_ref, pos_ref, start_ref, dest_ref, *, tm):
    lane = lax.broadcasted_iota(jnp.int32, (tm, N_EXPERTS), 1)
    out_lane = lax.broadcasted_iota(jnp.int32, (tm, 128), 1)
    idx = idx_ref[...]
    pos = pos_ref[...]
    start = start_ref[0:1, :]
    dest = jnp.zeros((tm, 128), F32)
    for r in range(TOP_K):
        base = jnp.sum(jnp.where(lane == idx[:, r:r + 1], start, 0.0), axis=-1, keepdims=True)
        dest = jnp.where(out_lane == r, base + pos[:, r:r + 1].astype(F32), dest)
    dest_ref[...] = dest.T[0:TOP_K, :].astype(jnp.int32)


def _slots(idx, pos, start):
    n = idx.shape[0]
    tm = TOK_TILE
    return pl.pallas_call(
        functools.partial(_slots_kernel, tm=tm),
        grid=(n // tm,),
        in_specs=[pl.BlockSpec((tm, 128), lambda i: (i, 0)), pl.BlockSpec((tm, 128), lambda i: (i, 0)),
                  pl.BlockSpec((8, N_EXPERTS), lambda i: (0, 0))],
        out_specs=pl.BlockSpec((TOP_K, tm), lambda i: (0, i)),
        out_shape=jax.ShapeDtypeStruct((TOP_K, n), jnp.int32),
        compiler_params=_cparams(1),
        name="slots",
    )(idx, pos, start)


def _dispatch_kernel(dest_ref, h_ref, xs_in_ref, xs_ref, sem, *, tm):
    del xs_in_ref

    def row_copy(r, k):
        return pltpu.make_async_copy(h_ref.at[pl.ds(r, 1), :],
                                     xs_ref.at[pl.ds(dest_ref[k, r], 1), :], sem)

    def start(r, c):
        for k in range(TOP_K):
            row_copy(r, k).start()
        return c

    def wait(r, c):
        for k in range(TOP_K):
            row_copy(r, k).wait()
        return c

    lax.fori_loop(0, tm, start, 0)
    lax.fori_loop(0, tm, wait, 0)


def _dispatch(dest, h2, xs_init):
    n, d = h2.shape
    tm = TOK_TILE
    return pl.pallas_call(
        functools.partial(_dispatch_kernel, tm=tm),
        grid=(n // tm,),
        in_specs=[pl.BlockSpec((TOP_K, tm), lambda i: (0, i), memory_space=pltpu.SMEM),
                  pl.BlockSpec((tm, d), lambda i: (i, 0)),
                  pl.BlockSpec(memory_space=pl.ANY)],
        out_specs=pl.BlockSpec(memory_space=pl.ANY),
        out_shape=jax.ShapeDtypeStruct(xs_init.shape, xs_init.dtype),
        scratch_shapes=[pltpu.SemaphoreType.DMA(())],
        input_output_aliases={2: 0},
        compiler_params=_cparams(1),
        name="dispatch",
    )(dest, h2, xs_init)


def _expert_kernel(blk_e_ref, n_used_ref, xs_ref, wi_ref, wo_ref, y_ref):
    del blk_e_ref
    used = pl.program_id(0) < n_used_ref[0]

    @pl.when(used)
    def _():
        ag = _dot(xs_ref[...].astype(BF16), wi_ref[0, 0].astype(BF16))
        a = ag[:, :E_HIDDEN]
        g = ag[:, E_HIDDEN:]
        y_ref[...] = _dot((a * _sigmoid(a) * g).astype(BF16), wo_ref[0, 0].astype(BF16))

    @pl.when(jnp.logical_not(used))
    def _():
        y_ref[...] = jnp.zeros_like(y_ref)


def _experts(blk_e, n_used, xs, w_e_in, w_e_out, layer):
    n_slots, d = xs.shape
    n_blk = n_slots // MOE_ROWS

    def blk(i, be, nu):
        return (jnp.minimum(i, nu[0] - 1), 0)

    def wsel(i, be, nu):
        return (layer, be[jnp.minimum(i, nu[0] - 1)], 0, 0)

    return pl.pallas_call(
        _expert_kernel,
        grid_spec=pltpu.PrefetchScalarGridSpec(
            num_scalar_prefetch=2,
            grid=(n_blk,),
            in_specs=[pl.BlockSpec((MOE_ROWS, d), blk),
                      pl.BlockSpec((1, 1, d, 2 * E_HIDDEN), wsel),
                      pl.BlockSpec((1, 1, E_HIDDEN, d), wsel)],
            out_specs=pl.BlockSpec((MOE_ROWS, d), lambda i, be, nu: (i, 0)),
        ),
        out_shape=jax.ShapeDtypeStruct((n_slots, d), F32),
        compiler_params=_cparams(1),
        name="experts",
    )(blk_e, n_used, xs, w_e_in, w_e_out)


def _combine_kernel(dest_ref, y_ref, h_ref, x1_ref, wn_ref, ml_ref, mc_ref, wsi_ref, wso_ref, ln_ref,
                    x2_ref, g_scr, sem, *, tm, n_lat, n_tok, alpha):
    def row_copy(r, k):
        return pltpu.make_async_copy(y_ref.at[pl.ds(dest_ref[k, r], 1), :],
                                     g_scr.at[k, pl.ds(r, 1), :], sem)

    def start(r, c):
        for k in range(TOP_K):
            row_copy(r, k).start()
        return c

    def wait(r, c):
        for k in range(TOP_K):
            row_copy(r, k).wait()
        return c

    lax.fori_loop(0, tm, start, 0)
    h = h_ref[...]
    ag = _dot(h.astype(BF16), wsi_ref[...])
    a = ag[:, :S_HIDDEN]
    g = ag[:, S_HIDDEN:]
    f = _dot((a * _sigmoid(a) * g).astype(BF16), wso_ref[...])
    lax.fori_loop(0, tm, wait, 0)
    wn = wn_ref[...]
    for k in range(TOP_K):
        f = f + wn[:, k:k + 1] * g_scr[k]
    rows = (pl.program_id(0) * tm) % n_tok + lax.broadcasted_iota(jnp.int32, (tm, 1), 0)
    is_ctx = rows >= n_lat
    u = alpha * x1_ref[...] + _pick_mod(is_ctx, mc_ref, ml_ref, 5) * f
    x2_ref[...] = _layer_norm(u, ln_ref[0:1, :], ln_ref[1:2, :])


def _combine(dest, y, h2, x1, wn, mod_lat, mod_ctx, ws_in, ws_out, ln2, *, n_lat, n_tok, alpha):
    n, d = h2.shape
    tm = TOK_TILE
    tiles_per_batch = n_tok // tm
    row = lambda i: (i, 0)
    const = lambda i: (0, 0)
    return pl.pallas_call(
        functools.partial(_combine_kernel, tm=tm, n_lat=n_lat, n_tok=n_tok, alpha=alpha),
        grid=(n // tm,),
        in_specs=[pl.BlockSpec((TOP_K, tm), lambda i: (0, i), memory_space=pltpu.SMEM),
                  pl.BlockSpec(memory_space=pl.ANY),
                  pl.BlockSpec((tm, d), row),
                  pl.BlockSpec((tm, d), row),
                  pl.BlockSpec((tm, 128), row),
                  pl.BlockSpec((1, 8, d), lambda i: (i // tiles_per_batch, 0, 0)),
                  pl.BlockSpec((8, d), const),
                  pl.BlockSpec((d, 2 * S_HIDDEN), const),
                  pl.BlockSpec((S_HIDDEN, d), const),
                  pl.BlockSpec((8, d), const)],
        out_specs=pl.BlockSpec((tm, d), row),
        out_shape=jax.ShapeDtypeStruct((n, d), F32),
        scratch_shapes=[pltpu.VMEM((TOP_K, tm, d), F32), pltpu.SemaphoreType.DMA(())],
        compiler_params=_cparams(1),
        name="combine",
    )(dest, y, h2, x1, wn, mod_lat, mod_ctx, ws_in, ws_out, ln2)


def _rope_tables(n_lat, n_ctx):
    rows = n_lat // GRID_W
    row = jnp.repeat(jnp.arange(rows, dtype=F32), GRID_W)
    col = jnp.tile(jnp.arange(GRID_W, dtype=F32), rows)
    n_freq = A_DIM // 4
    inv = jnp.power(ROPE_THETA, -jnp.arange(n_freq, dtype=F32) / n_freq)
    ang_r = row[:, None] * inv
    ang_c = col[:, None] * inv
    cos = jnp.concatenate([jnp.cos(ang_r), jnp.cos(ang_r), jnp.cos(ang_c), jnp.cos(ang_c)], axis=1)
    sin = jnp.concatenate([-jnp.sin(ang_r), jnp.sin(ang_r), -jnp.sin(ang_c), jnp.sin(ang_c)], axis=1)
    cos = jnp.concatenate([jnp.tile(cos, (1, 2)), jnp.ones((n_ctx, 128), F32)], axis=0)
    sin = jnp.concatenate([jnp.tile(sin, (1, 2)), jnp.zeros((n_ctx, 128), F32)], axis=0)
    return cos, sin


def _row_tile(n_tok):
    for tm in (1280, 640, 256):
        if n_tok % tm == 0:
            return tm
    raise ValueError(f"unsupported token count {n_tok}")


def _pad_rows(a, rows):
    return jnp.concatenate([a, jnp.zeros((rows - a.shape[0],) + a.shape[1:], a.dtype)], axis=0)


def kernel(x, c, ctx, c_ctx, ada_w, ada_b, w_in, b_gates, conv_qk, lam, subln_a, norm_m, w_br_a, w_br_m, w_out,
           ln1_g, ln1_b, ln2_g, ln2_b, w_router, e_bias, w_e_in, w_e_out, ws_in, ws_out):
    bsz, n_lat, d = x.shape
    n_ctx = ctx.shape[1]
    n_tok = n_lat + n_ctx
    depth = ada_w.shape[0]
    assert d == D_MODEL and n_ctx == MLSTM_CHUNK and n_lat % 512 == 0 and bsz + 1 <= 8
    alpha = (2 * depth) ** 0.25
    tm = _row_tile(n_tok)
    tq = 512
    ck = next(c for c in (1280, 256) if n_tok % c == 0)
    n_all = bsz * n_tok
    n_assign = n_all * TOP_K
    n_blk = -(-n_assign // MOE_ROWS) + N_EXPERTS
    n_slots = n_blk * MOE_ROWS

    cond = _pad_rows(jnp.concatenate([c, c_ctx[None, :]], axis=0), 8)
    mods = _ada(cond, ada_w, ada_b).reshape(depth, 8, N_MOD, d)
    cos_t, sin_t = _rope_tables(n_lat, n_ctx)
    xc = jnp.concatenate([x, ctx], axis=1)
    g_lo = 6 * COL_TILE
    g_hi = g_lo + 4 * M_HEADS

    for l in range(depth):
        lam_init = 0.8 - 0.6 * math.exp(-0.3 * l)
        mod_lat = _pad_rows(mods[l, :bsz].transpose(1, 0, 2), 8).transpose(1, 0, 2)
        mod_ctx = _pad_rows(mods[l, bsz], 8)
        w_main = jnp.concatenate([w_in[l, :, :g_lo], w_in[l, :, g_hi:]], axis=1).astype(BF16)
        w_gates = jnp.pad(w_in[l, :, g_lo:g_hi], ((0, 0), (0, 128 - 4 * M_HEADS)))
        bg = jnp.pad(b_gates[l], (0, 128 - 4 * M_HEADS))[None, :]
        conv_w = _pad_rows(conv_qk[l], 8)
        lam_p = jnp.pad(lam[l], ((0, 4), (0, 128 - A_DIM)))
        subln = subln_a[l][None, :]

        z, gates = _proj(xc, mod_lat, mod_ctx, w_main, w_gates, bg, cos_t, sin_t, conv_w, n_lat=n_lat, tm=tm)
        oa_lat = _attention(z, lam_p, subln, n_lat=n_lat, tq=tq, ck=ck, lat_queries=True, lam_init=lam_init)
        oa_ctx = _attention(z, lam_p, subln, n_lat=n_lat, tq=n_ctx, ck=ck, lat_queries=False, lam_init=lam_init)
        gates_t = gates[:, :, :4 * M_HEADS].transpose(0, 2, 1)
        hf, hb = _mlstm(z, gates, gates_t)
        ln1 = _pad_rows(jnp.stack([ln1_g[l], ln1_b[l]]), 8)
        x1, h2 = _merge(xc, oa_lat, oa_ctx, hf, hb, z, mod_lat, mod_ctx, norm_m[l][None, :], w_br_a[l].astype(BF16),
                        w_br_m[l].astype(BF16), w_out[l].astype(BF16), ln1, n_lat=n_lat, tm=TOK_TILE, alpha=alpha)

        h2f = h2.reshape(n_all, d)
        idx, wn, pos, cnt = _router(h2f, w_router[l], e_bias[l][None, :])
        counts = cnt[0]
        padded = (counts + MOE_ROWS - 1) // MOE_ROWS * MOE_ROWS
        pend = jnp.cumsum(padded)
        pstart = pend - padded
        dest = _slots(idx, pos, jnp.broadcast_to(pstart.astype(F32)[None, :], (8, N_EXPERTS)))
        blk_e = jnp.minimum(jnp.searchsorted(pend, jnp.arange(n_blk, dtype=jnp.int32) * MOE_ROWS, side="right"),
                            N_EXPERTS - 1).astype(jnp.int32)
        n_used = (pend[-1:] // MOE_ROWS).astype(jnp.int32)
        xs = _dispatch(dest, h2f, jnp.zeros((n_slots, d), F32))
        y = _experts(blk_e, n_used, xs, w_e_in, w_e_out, l)
        ln2 = _pad_rows(jnp.stack([ln2_g[l], ln2_b[l]]), 8)
        x2 = _combine(dest, y, h2f, x1.reshape(n_all, d), wn, mod_lat, mod_ctx, ws_in[l].astype(BF16),
                      ws_out[l].astype(BF16), ln2, n_lat=n_lat, n_tok=n_tok, alpha=alpha)
        xc = x2.reshape(bsz, n_tok, d)
    return xc[:, :n_lat]
```

```python
import functools
import math

import jax
import jax.numpy as jnp
from jax import lax
from jax.experimental import pallas as pl
from jax.experimental.pallas import tpu as pltpu

F32 = jnp.float32
BF16 = jnp.bfloat16

D_MODEL = 1024
GRID_W = 64
A_HEADS = 8
A_DIM = 64
A_VDIM = 2 * A_DIM
ROPE_THETA = 10000.0
Q_SCALE = A_DIM ** -0.5 * math.log2(math.e)
M_HEADS = 4
M_QK = 128
M_V = 256
M_VAUG = M_V + 128
N_EXPERTS = 256
TOP_K = 8
E_HIDDEN = 256
S_HIDDEN = 256
ROUTE_SCALE = 2.5
LN_EPS = 1e-5
RMS_EPS = 1e-6
N_MOD = 6
COL_TILE = 1024
N_COL_TILES = 8
ATTN_UNROLL = 4
MLSTM_CHUNK = 256
MOE_ROWS = 256
TOK_TILE = 256
VMEM_LIMIT = 56 * 1024 * 1024


def _cparams(n_axes):
    return pltpu.CompilerParams(dimension_semantics=("arbitrary",) * n_axes,
                                vmem_limit_bytes=VMEM_LIMIT)


def _dot(a, b):
    return jnp.dot(a, b, preferred_element_type=F32)


def _dot_nt(a, b):
    return lax.dot_general(a, b, (((1,), (1,)), ((), ())), preferred_element_type=F32)


def _split_bf16(a):
    hi = a.astype(BF16)
    lo = (a - hi.astype(F32)).astype(BF16)
    return hi, lo


def _dot3(a, b):
    ah, al = _split_bf16(a)
    bh, bl = _split_bf16(b)
    return _dot(ah, bh) + _dot(al, bh) + _dot(ah, bl)


def _pack_bf16(x):
    n = x.shape[1] // 2
    hi = lax.bitcast_convert_type(x[:, :n].astype(BF16).astype(F32), jnp.uint32)
    lo = lax.bitcast_convert_type(x[:, n:].astype(BF16).astype(F32), jnp.uint32)
    return hi | (lo >> 16)


def _unpack_bf16(p):
    hi = lax.bitcast_convert_type(p & jnp.uint32(0xFFFF0000), F32)
    lo = lax.bitcast_convert_type(p << 16, F32)
    return hi, lo


def _sigmoid(v):
    return 1.0 / (1.0 + jnp.exp(-v))


def _log_sigmoid(v):
    return jnp.minimum(v, 0.0) - jnp.log(1.0 + jnp.exp(-jnp.abs(v)))


def _layer_norm(u, g, b):
    mu = jnp.mean(u, axis=-1, keepdims=True)
    var = jnp.mean(jnp.square(u - mu), axis=-1, keepdims=True)
    return (u - mu) * lax.rsqrt(var + LN_EPS) * g + b


def _pick_mod(is_ctx, mc_ref, ml_ref, k):
    return jnp.where(is_ctx, mc_ref[k:k + 1, :], ml_ref[0, k:k + 1, :])


def _ada_kernel(cond_ref, w_ref, b_ref, o_ref):
    c = cond_ref[...]
    c = c * _sigmoid(c)
    o_ref[0] = _dot3(c, w_ref[0]) + b_ref[0]


def _ada(cond, ada_w, ada_b):
    depth, d, width = ada_w.shape
    tn = 512
    return pl.pallas_call(
        _ada_kernel,
        grid=(depth, width // tn),
        in_specs=[pl.BlockSpec((8, d), lambda l, j: (0, 0)),
                  pl.BlockSpec((1, d, tn), lambda l, j: (l, 0, j)),
                  pl.BlockSpec((1, 1, tn), lambda l, j: (l, 0, j))],
        out_specs=pl.BlockSpec((1, 8, tn), lambda l, j: (l, 0, j)),
        out_shape=jax.ShapeDtypeStruct((depth, 8, width), F32),
        compiler_params=_cparams(2),
        name="ada",
    )(cond, ada_w, ada_b.reshape(depth, 1, width))


def _proj_kernel(x_ref, xp_ref, xn_ref, ml_ref, mc_ref, w_ref, wg_ref, bg_ref, cos_ref, sin_ref,
                 conv_ref, z_ref, gates_ref, h_scr, halo_scr, *, tm, n_lat, n_tok):
    i = pl.program_id(1)
    j = pl.program_id(2)
    row0 = i * tm

    def modulate(xv, rows):
        is_ctx = rows >= n_lat
        return xv * (1.0 + _pick_mod(is_ctx, mc_ref, ml_ref, 1)) + _pick_mod(is_ctx, mc_ref, ml_ref, 0)

    @pl.when(j == 0)
    def _():
        rows = row0 + lax.broadcasted_iota(jnp.int32, (tm, 1), 0)
        h = modulate(x_ref[0], rows)
        h_scr[...] = h.astype(BF16)
        r8 = lax.broadcasted_iota(jnp.int32, (8, 1), 0)
        halo_scr[0:8, :] = modulate(xp_ref[0], row0 - 8 + r8)
        halo_scr[8:16, :] = modulate(xn_ref[0], row0 + tm + r8)
        gates_ref[0] = _dot3(h, wg_ref[...]) + bg_ref[...]

    zt = _dot(h_scr[...], w_ref[...])

    @pl.when(j <= 1)
    def _():
        lane = lax.broadcasted_iota(jnp.int32, (tm, 128), 1)
        first = (lane % 32) < 16
        scale = jnp.where(j == 0, Q_SCALE, 1.0).astype(F32)
        cos = cos_ref[...] * scale
        sin = sin_ref[...] * scale
        for g in range(COL_TILE // 128):
            zg = zt[:, g * 128:(g + 1) * 128]
            partner = jnp.where(first, pltpu.roll(zg, 128 - 16, 1), pltpu.roll(zg, 16, 1))
            z_ref[0, :, g * 128:(g + 1) * 128] = (zg * cos + partner * sin).astype(BF16)

    @pl.when(j == 3)
    def _():
        zh = _dot(halo_scr[...].astype(BF16), w_ref[...])
        loc = lax.broadcasted_iota(jnp.int32, (tm, 1), 0)
        rows = row0 + loc
        seg_start = (rows == 0) | (rows == n_lat)
        seg_end = (rows == n_lat - 1) | (rows == n_tok - 1)
        zd = jnp.where(loc == 0, zh[7:8, :], pltpu.roll(zt, 1, 0))
        zd = jnp.where(seg_start, 0.0, zd)
        zu = jnp.where(loc == tm - 1, zh[8:9, :], pltpu.roll(zt, tm - 1, 0))
        zu = jnp.where(seg_end, 0.0, zu)
        y = zd * conv_ref[0:1, :] + zt * conv_ref[1:2, :] + zu * conv_ref[2:3, :]
        y = y * _sigmoid(y)
        lane = lax.broadcasted_iota(jnp.int32, (1, COL_TILE), 1)
        y = y * jnp.where(lane >= M_HEADS * M_QK, M_QK ** -0.5, 1.0).astype(F32)
        z_ref[0] = y.astype(BF16)

    @pl.when((j == 2) | (j >= 4))
    def _():
        z_ref[0] = zt.astype(BF16)


def _proj(xc, mod_lat, mod_ctx, w_main, w_gates, b_gates, cos_t, sin_t, conv_w, *, n_lat, tm):
    bsz, n_tok, d = xc.shape
    nt8 = n_tok // 8
    kern = functools.partial(_proj_kernel, tm=tm, n_lat=n_lat, n_tok=n_tok)
    return pl.pallas_call(
        kern,
        grid=(bsz, n_tok // tm, N_COL_TILES),
        in_specs=[
            pl.BlockSpec((1, tm, d), lambda b, i, j: (b, i, 0)),
            pl.BlockSpec((1, 8, d), lambda b, i, j: (b, jnp.maximum(i * (tm // 8) - 1, 0), 0)),
            pl.BlockSpec((1, 8, d), lambda b, i, j: (b, jnp.minimum((i + 1) * (tm // 8), nt8 - 1), 0)),
            pl.BlockSpec((1, 8, d), lambda b, i, j: (b, 0, 0)),
            pl.BlockSpec((8, d), lambda b, i, j: (0, 0)),
            pl.BlockSpec((d, COL_TILE), lambda b, i, j: (0, j)),
            pl.BlockSpec((d, 128), lambda b, i, j: (0, 0)),
            pl.BlockSpec((1, 128), lambda b, i, j: (0, 0)),
            pl.BlockSpec((tm, 128), lambda b, i, j: (i, 0)),
            pl.BlockSpec((tm, 128), lambda b, i, j: (i, 0)),
            pl.BlockSpec((8, COL_TILE), lambda b, i, j: (0, 0)),
        ],
        out_specs=[
            pl.BlockSpec((1, tm, COL_TILE), lambda b, i, j: (b, i, j)),
            pl.BlockSpec((1, tm, 128), lambda b, i, j: (b, i, 0)),
        ],
        out_shape=[jax.ShapeDtypeStruct((bsz, n_tok, N_COL_TILES * COL_TILE), BF16),
                   jax.ShapeDtypeStruct((bsz, n_tok, 128), F32)],
        scratch_shapes=[pltpu.VMEM((tm, d), BF16), pltpu.VMEM((16, d), F32)],
        compiler_params=_cparams(3),
        name="proj",
    )(xc, xc, xc, mod_lat, mod_ctx, w_main, w_gates, b_gates, cos_t, sin_t, conv_w)


def _attn_kernel(lam_ref, g_ref, q_ref, k_ref, v_ref, o_ref, *s_scr, tq, ck, n_chunks, kv_lo, lam_init):
    q = q_ref[0]
    lane = lax.broadcasted_iota(jnp.int32, (1, A_VDIM), 1)
    zero = jnp.zeros_like(q)
    qs = (jnp.where(lane < A_DIM, q, zero), jnp.where(lane >= A_DIM, q, zero))
    slots = (s_scr[0:2], s_scr[2:4])

    ones_blk = jnp.where(lax.broadcasted_iota(jnp.int32, (ck, A_VDIM), 1) == 0, 1.0, 0.0).astype(BF16)

    def kv_start(c):
        return pl.multiple_of(kv_lo + c * ck, 128)

    def scores(c, slot):
        k = k_ref[0, pl.ds(kv_start(c), ck), :]
        for qj, s_ref in zip(qs, slots[slot]):
            s_ref[...] = _dot_nt(qj, k)

    def absorb(c, slot, carry):
        v_aug = jnp.concatenate([v_ref[0, pl.ds(kv_start(c), ck), :], ones_blk], axis=1)
        out = []
        for s_ref, (m, acc) in zip(slots[slot], carry):
            s = s_ref[...]
            m_new = jnp.maximum(m, jnp.max(s, axis=-1, keepdims=True))
            alpha = jnp.exp2(m - m_new)
            p = jnp.exp2(s - m_new)
            acc = alpha * acc + _dot(p.astype(BF16), v_aug)
            out.append((m_new, acc))
        return tuple(out)

    def run(c0, count, carry):
        for u in range(count):
            scores(c0 + u + 1, (u + 1) % 2)
            carry = absorb(c0 + u, u % 2, carry)
        return carry

    one = (jnp.full((tq, 1), -jnp.inf, F32), jnp.zeros((tq, 2 * A_VDIM), F32))
    scores(0, 0)
    trips = (n_chunks - 1) // ATTN_UNROLL
    carry = (one, one)
    if trips > 0:
        carry = lax.fori_loop(0, trips, lambda i, cr: run(i * ATTN_UNROLL, ATTN_UNROLL, cr), carry)
    rest = n_chunks - 1 - trips * ATTN_UNROLL
    carry = run(trips * ATTN_UNROLL, rest, carry)
    carry = absorb(n_chunks - 1, rest % 2, carry)

    lam = lam_ref[...]
    s1 = jnp.sum(lam[0:1, :] * lam[1:2, :], axis=-1, keepdims=True)
    s2 = jnp.sum(lam[2:3, :] * lam[3:4, :], axis=-1, keepdims=True)
    lam_val = jnp.exp(s1) - jnp.exp(s2) + lam_init
    (_, acc0), (_, acc1) = carry
    o = (acc0[:, :A_VDIM] / acc0[:, A_VDIM:A_VDIM + 1]
         - lam_val * (acc1[:, :A_VDIM] / acc1[:, A_VDIM:A_VDIM + 1]))
    y = o * lax.rsqrt(jnp.mean(o * o, axis=-1, keepdims=True) + RMS_EPS) * g_ref[...] * (1.0 - lam_init)
    o_ref[0] = y.astype(BF16)


def _attention(z, lam_p, subln, *, n_lat, tq, ck, lat_queries, lam_init):
    bsz, n_tok, _ = z.shape
    n_ctx = n_tok - n_lat
    n_q = n_lat if lat_queries else n_ctx
    qb = 0 if lat_queries else n_lat // tq
    if not lat_queries:
        ck = n_ctx
    kern = functools.partial(_attn_kernel, tq=tq, ck=ck, n_chunks=(n_tok if lat_queries else n_ctx) // ck,
                             kv_lo=0 if lat_queries else n_lat, lam_init=lam_init)
    return pl.pallas_call(
        kern,
        grid=(bsz, A_HEADS, n_q // tq),
        in_specs=[
            pl.BlockSpec((8, 128), lambda b, h, i: (0, 0)),
            pl.BlockSpec((1, 128), lambda b, h, i: (0, 0)),
            pl.BlockSpec((1, tq, A_VDIM), lambda b, h, i: (b, qb + i, h)),
            pl.BlockSpec((1, n_tok, A_VDIM), lambda b, h, i: (b, 0, A_HEADS + h)),
            pl.BlockSpec((1, n_tok, A_VDIM), lambda b, h, i: (b, 0, 2 * A_HEADS + h)),
        ],
        out_specs=pl.BlockSpec((1, tq, A_VDIM), lambda b, h, i: (b, i, h)),
        out_shape=jax.ShapeDtypeStruct((bsz, n_q, A_HEADS * A_VDIM), BF16),
        scratch_shapes=[pltpu.VMEM((tq, ck), F32)] * 4,
        compiler_params=_cparams(3),
        name="attn_lat" if lat_queries else "attn_ctx",
    )(lam_p, subln, z, z, z)


def _mlstm_kernel(qf_ref, kf_ref, vf_ref, gf_ref, gtf_ref, qb_ref, kb_ref, vb_ref, gb_ref, gtb_ref,
                  hf_ref, hb_ref, c_scr, m_scr, *, t):
    step = pl.program_id(1)

    @pl.when(step == 0)
    def _():
        c_scr[...] = jnp.zeros_like(c_scr)
        m_scr[...] = jnp.zeros_like(m_scr)

    ti = lax.broadcasted_iota(jnp.int32, (t, t), 0)
    si = lax.broadcasted_iota(jnp.int32, (t, t), 1)
    lane = lax.broadcasted_iota(jnp.int32, (t, 128), 1)
    ones_blk = jnp.where(lane == 0, 1.0, 0.0).astype(BF16)
    dirs = ((qf_ref, kf_ref, vf_ref, gf_ref, gtf_ref, hf_ref), (qb_ref, kb_ref, vb_ref, gb_ref, gtb_ref, hb_ref))
    for d, (q_ref, k_ref, v_ref, g_ref, gt_ref, h_ref) in enumerate(dirs):
        seen = (si <= ti) if d == 0 else (si >= ti)
        seen_t = (ti <= si) if d == 0 else (ti >= si)
        for hd in range(M_HEADS):
            idx = d * M_HEADS + hd
            q = q_ref[0, :, hd * M_QK:(hd + 1) * M_QK]
            k = k_ref[0, :, hd * M_QK:(hd + 1) * M_QK]
            v = v_ref[0, :, hd * M_V:(hd + 1) * M_V]
            li = d * 2 * M_HEADS + hd
            lf = li + M_HEADS
            i_col = g_ref[0, :, li:li + 1]
            i_row = gt_ref[0, li:li + 1, :]
            lf_col = _log_sigmoid(g_ref[0, :, lf:lf + 1])
            lf_row = _log_sigmoid(gt_ref[0, lf:lf + 1, :])
            b_col = jnp.sum(jnp.where(seen, lf_row, 0.0), axis=1, keepdims=True)
            b_row = jnp.sum(jnp.where(seen_t, lf_col, 0.0), axis=0, keepdims=True)
            b_end = jnp.sum(lf_row, axis=1, keepdims=True)
            m_old = m_scr[idx, 0:1, 0:1]
            dmat = jnp.where(seen, b_col - b_row + i_row, -jnp.inf)
            m_t = jnp.maximum(b_col + m_old, jnp.max(dmat, axis=1, keepdims=True))
            dexp = jnp.exp(dmat - m_t)
            iscale = jnp.exp(b_col + m_old - m_t)
            sc = (_dot_nt(q, k) * dexp).astype(BF16)
            c_aug = c_scr[idx]
            v_aug = jnp.concatenate([v, ones_blk], axis=1)
            inter = _dot(q, c_aug.astype(BF16))
            intra = _dot(sc, v_aug)
            num = iscale * inter[:, :M_V] + intra[:, :M_V]
            den = iscale * inter[:, M_V:M_V + 1] + intra[:, M_V:M_V + 1]
            h_ref[0, :, hd * M_V:(hd + 1) * M_V] = num / jnp.maximum(jnp.abs(den), jnp.exp(-m_t))
            g_col = b_end - b_col + i_col
            m_new = jnp.maximum(b_end + m_old, jnp.max(g_col, axis=0, keepdims=True))
            w_col = jnp.exp(g_col - m_new)
            decay = jnp.exp(b_end + m_old - m_new)
            wv = (w_col * v_aug.astype(F32)).astype(BF16)
            k_t = k.astype(F32).T.astype(BF16)
            c_scr[idx] = decay * c_aug + _dot(k_t, wv)
            m_scr[idx] = jnp.broadcast_to(m_new, (8, 128))


def _mlstm(z, gates, gates_t):
    bsz, n_tok, _ = z.shape
    t = MLSTM_CHUNK
    nc = n_tok // t
    qcol = 3 * COL_TILE // (M_HEADS * M_QK)

    def cf(s):
        return jnp.where(s == 0, nc - 1, s - 1)

    def cb(s):
        return nc - 1 - s

    def specs(c):
        return [
            pl.BlockSpec((1, t, M_HEADS * M_QK), lambda b, s: (b, c(s), qcol)),
            pl.BlockSpec((1, t, M_HEADS * M_QK), lambda b, s: (b, c(s), qcol + 1)),
            pl.BlockSpec((1, t, M_HEADS * M_V), lambda b, s: (b, c(s), 4)),
            pl.BlockSpec((1, t, 128), lambda b, s: (b, c(s), 0)),
            pl.BlockSpec((1, 16, t), lambda b, s: (b, 0, c(s))),
        ]

    out_sd = jax.ShapeDtypeStruct((bsz, n_tok, M_HEADS * M_V), F32)
    return pl.pallas_call(
        functools.partial(_mlstm_kernel, t=t),
        grid=(bsz, nc),
        in_specs=specs(cf) + specs(cb),
        out_specs=[pl.BlockSpec((1, t, M_HEADS * M_V), lambda b, s: (b, cf(s), 0)),
                   pl.BlockSpec((1, t, M_HEADS * M_V), lambda b, s: (b, cb(s), 0))],
        out_shape=[out_sd, out_sd],
        scratch_shapes=[pltpu.VMEM((2 * M_HEADS, M_QK, M_VAUG), F32), pltpu.VMEM((2 * M_HEADS, 8, 128), F32)],
        compiler_params=_cparams(2),
        name="mlstm",
    )(z, z, z, gates, gates_t, z, z, z, gates, gates_t)


def _merge_kernel(x_ref, oal_ref, oac_ref, hf_ref, hb_ref, om_ref, ga_ref, gb_ref, ml_ref, mc_ref, nm_ref,
                  wa_ref, wm_ref, wo_ref, ln_ref, x1_ref, h2_ref, *, tm, n_lat, alpha):
    i = pl.program_id(1)
    rows = i * tm + lax.broadcasted_iota(jnp.int32, (tm, 1), 0)
    is_ctx = rows >= n_lat
    oa = jnp.where(i * tm >= n_lat, oac_ref[0], oal_ref[0])
    hm = hf_ref[0] + hb_ref[0]
    parts = []
    for hd in range(M_HEADS):
        seg = hm[:, hd * M_V:(hd + 1) * M_V]
        parts.append(seg * lax.rsqrt(jnp.mean(seg * seg, axis=-1, keepdims=True) + RMS_EPS))
    hm = jnp.concatenate(parts, axis=1) * nm_ref[...] * _sigmoid(om_ref[0].astype(F32))
    ya = _dot(oa, wa_ref[...])
    ym = _dot(hm.astype(BF16), wm_ref[...])
    y = _sigmoid(ga_ref[0].astype(F32)) * ya + _sigmoid(gb_ref[0].astype(F32)) * ym
    y = _dot(y.astype(BF16), wo_ref[...])
    u = alpha * x_ref[0] + _pick_mod(is_ctx, mc_ref, ml_ref, 2) * y
    x1 = _layer_norm(u, ln_ref[0:1, :], ln_ref[1:2, :])
    x1_ref[0] = x1
    h2_ref[0] = x1 * (1.0 + _pick_mod(is_ctx, mc_ref, ml_ref, 4)) + _pick_mod(is_ctx, mc_ref, ml_ref, 3)


def _merge(xc, oa_lat, oa_ctx, hf, hb, z, mod_lat, mod_ctx, norm_m, w_a, w_m, w_o, ln1, *, n_lat, tm, alpha):
    bsz, n_tok, d = xc.shape
    row = lambda b, i: (b, i, 0)
    const = lambda b, i: (0, 0)
    out_sd = jax.ShapeDtypeStruct((bsz, n_tok, d), F32)
    lat_tiles = n_lat // tm
    return pl.pallas_call(
        functools.partial(_merge_kernel, tm=tm, n_lat=n_lat, alpha=alpha),
        grid=(bsz, n_tok // tm),
        in_specs=[
            pl.BlockSpec((1, tm, d), row),
            pl.BlockSpec((1, tm, d), lambda b, i: (b, jnp.minimum(i, lat_tiles - 1), 0)),
            pl.BlockSpec((1, tm, d), lambda b, i: (b, jnp.maximum(i - lat_tiles, 0), 0)),
            pl.BlockSpec((1, tm, d), row),
            pl.BlockSpec((1, tm, d), row),
            pl.BlockSpec((1, tm, COL_TILE), lambda b, i: (b, i, 5)),
            pl.BlockSpec((1, tm, COL_TILE), lambda b, i: (b, i, 6)),
            pl.BlockSpec((1, tm, COL_TILE), lambda b, i: (b, i, 7)),
            pl.BlockSpec((1, 8, d), lambda b, i: (b, 0, 0)),
            pl.BlockSpec((8, d), const),
            pl.BlockSpec((1, d), const),
            pl.BlockSpec((d, d), const),
            pl.BlockSpec((d, d), const),
            pl.BlockSpec((d, d), const),
            pl.BlockSpec((8, d), const),
        ],
        out_specs=[pl.BlockSpec((1, tm, d), row), pl.BlockSpec((1, tm, d), row)],
        out_shape=[out_sd, out_sd],
        compiler_params=_cparams(2),
        name="merge",
    )(xc, oa_lat, oa_ctx, hf, hb, z, z, z, mod_lat, mod_ctx, norm_m, w_a, w_m, w_o, ln1)


def _router_kernel(h_ref, wr_ref, eb_ref, idx_ref, wn_ref, pos_ref, cnt_ref, carry_scr, *, tm):
    @pl.when(pl.program_id(0) == 0)
    def _():
        carry_scr[...] = jnp.zeros_like(carry_scr)

    scores = _sigmoid(_dot3(h_ref[...], wr_ref[...]))
    biased = scores + eb_ref[...]
    lane = lax.broadcasted_iota(jnp.int32, (tm, N_EXPERTS), 1).astype(F32)
    hots, sels, ws = [], [], []
    for _ in range(TOP_K):
        mx = jnp.max(biased, axis=-1, keepdims=True)
        sel = jnp.min(jnp.where(biased == mx, lane, float(N_EXPERTS)), axis=-1, keepdims=True)
        hot = lane == sel
        hots.append(hot)
        sels.append(sel)
        ws.append(jnp.sum(jnp.where(hot, scores, 0.0), axis=-1, keepdims=True))
        biased = jnp.where(hot, -jnp.inf, biased)
    wsum = ws[0]
    chosen = hots[0]
    for r in range(1, TOP_K):
        wsum = wsum + ws[r]
        chosen = chosen | hots[r]
    chosen_f = jnp.where(chosen, 1.0, 0.0)
    ti = lax.broadcasted_iota(jnp.int32, (tm, tm), 0)
    si = lax.broadcasted_iota(jnp.int32, (tm, tm), 1)
    before = jnp.where(si < ti, 1.0, 0.0).astype(BF16)
    rank = carry_scr[0:1, :] + _dot(before, chosen_f.astype(BF16))
    out_lane = lax.broadcasted_iota(jnp.int32, (tm, 128), 1)
    idx_o = jnp.zeros((tm, 128), jnp.int32)
    pos_o = jnp.zeros((tm, 128), jnp.int32)
    wn_o = jnp.zeros((tm, 128), F32)
    for r in range(TOP_K):
        pos_r = jnp.sum(jnp.where(hots[r], rank, 0.0), axis=-1, keepdims=True).astype(jnp.int32)
        idx_o = jnp.where(out_lane == r, sels[r].astype(jnp.int32), idx_o)
        pos_o = jnp.where(out_lane == r, pos_r, pos_o)
        wn_o = jnp.where(out_lane == r, ws[r] / wsum * ROUTE_SCALE, wn_o)
    idx_ref[...] = idx_o
    pos_ref[...] = pos_o
    wn_ref[...] = wn_o
    total = carry_scr[0:1, :] + jnp.sum(chosen_f, axis=0, keepdims=True)
    carry_scr[...] = jnp.broadcast_to(total, carry_scr.shape)
    cnt_ref[...] = jnp.broadcast_to(total, cnt_ref.shape).astype(jnp.int32)


def _router(h2, w_router, e_bias):
    n, d = h2.shape
    tm = TOK_TILE
    row = lambda i: (i, 0)
    const = lambda i: (0, 0)
    return pl.pallas_call(
        functools.partial(_router_kernel, tm=tm),
        grid=(n // tm,),
        in_specs=[pl.BlockSpec((tm, d), row), pl.BlockSpec((d, N_EXPERTS), const),
                  pl.BlockSpec((1, N_EXPERTS), const)],
        out_specs=[pl.BlockSpec((tm, 128), row), pl.BlockSpec((tm, 128), row), pl.BlockSpec((tm, 128), row),
                   pl.BlockSpec((8, N_EXPERTS), const)],
        out_shape=[jax.ShapeDtypeStruct((n, 128), jnp.int32), jax.ShapeDtypeStruct((n, 128), F32),
                   jax.ShapeDtypeStruct((n, 128), jnp.int32), jax.ShapeDtypeStruct((8, N_EXPERTS), jnp.int32)],
        scratch_shapes=[pltpu.VMEM((8, N_EXPERTS), F32)],
        compiler_params=_cparams(1),
        name="router",
    )(h2, w_router, e_bias)


def _slots_kernel(idx_ref, pos_ref, start_ref, dest_ref, *, tm):
    lane = lax.broadcasted_iota(jnp.int32, (tm, N_EXPERTS), 1)
    out_lane = lax.broadcasted_iota(jnp.int32, (tm, 128), 1)
    idx = idx_ref[...]
    pos = pos_ref[...]
    start = start_ref[0:1, :]
    dest = jnp.zeros((tm, 128), F32)
    for r in range(TOP_K):
        base = jnp.sum(jnp.where(lane == idx[:, r:r + 1], start, 0.0), axis=-1, keepdims=True)
        dest = jnp.where(out_lane == r, base + pos[:, r:r + 1].astype(F32), dest)
    dest_ref[...] = dest.T[0:TOP_K, :].astype(jnp.int32)


def _slots(idx, pos, start):
    n = idx.shape[0]
    tm = TOK_TILE
    return pl.pallas_call(
        functools.partial(_slots_kernel, tm=tm),
        grid=(n // tm,),
        in_specs=[pl.BlockSpec((tm, 128), lambda i: (i, 0)), pl.BlockSpec((tm, 128), lambda i: (i, 0)),
                  pl.BlockSpec((8, N_EXPERTS), lambda i: (0, 0))],
        out_specs=pl.BlockSpec((TOP_K, tm), lambda i: (0, i)),
        out_shape=jax.ShapeDtypeStruct((TOP_K, n), jnp.int32),
        compiler_params=_cparams(1),
        name="slots",
    )(idx, pos, start)


def _dispatch_kernel(dest_ref, h_ref, xs_in_ref, xs_ref, pk_scr, sem, *, tm):
    del xs_in_ref
    pk_scr[...] = _pack_bf16(h_ref[...])

    def start(r, c):
        for k in range(TOP_K):
            pltpu.make_async_copy(pk_scr.at[pl.ds(r, 1), :], xs_ref.at[pl.ds(dest_ref[k, r], 1), :], sem).start()
        return c

    lax.fori_loop(0, tm, start, 0)
    for k in range(TOP_K):
        pltpu.make_async_copy(pk_scr, xs_ref.at[pl.ds(0, tm), :], sem).wait()


def _dispatch(dest, h2, xs_init):
    n, d = h2.shape
    tm = TOK_TILE
    return pl.pallas_call(
        functools.partial(_dispatch_kernel, tm=tm),
        grid=(n // tm,),
        in_specs=[pl.BlockSpec((TOP_K, tm), lambda i: (0, i), memory_space=pltpu.SMEM),
                  pl.BlockSpec((tm, d), lambda i: (i, 0)),
                  pl.BlockSpec(memory_space=pl.ANY)],
        out_specs=pl.BlockSpec(memory_space=pl.ANY),
        out_shape=jax.ShapeDtypeStruct(xs_init.shape, xs_init.dtype),
        scratch_shapes=[pltpu.VMEM((tm, d // 2), jnp.uint32), pltpu.SemaphoreType.DMA(())],
        input_output_aliases={2: 0},
        compiler_params=_cparams(1),
        name="dispatch",
    )(dest, h2, xs_init)


def _expert_kernel(blk_e_ref, n_used_ref, xs_ref, wi_ref, wo_ref, y_ref):
    del blk_e_ref
    used = pl.program_id(0) < n_used_ref[0]

    @pl.when(used)
    def _():
        x = jnp.concatenate(_unpack_bf16(xs_ref[...]), axis=1).astype(BF16)
        ag = _dot(x, wi_ref[0, 0].astype(BF16))
        a = ag[:, :E_HIDDEN]
        g = ag[:, E_HIDDEN:]
        y_ref[...] = _pack_bf16(_dot((a * _sigmoid(a) * g).astype(BF16), wo_ref[0, 0].astype(BF16)))

    @pl.when(jnp.logical_not(used))
    def _():
        y_ref[...] = jnp.zeros_like(y_ref)


def _experts(blk_e, n_used, xs, w_e_in, w_e_out, layer):
    n_slots, dp = xs.shape
    d = 2 * dp
    n_blk = n_slots // MOE_ROWS

    def blk(i, be, nu):
        return (jnp.minimum(i, nu[0] - 1), 0)

    def wsel(i, be, nu):
        return (layer, be[jnp.minimum(i, nu[0] - 1)], 0, 0)

    return pl.pallas_call(
        _expert_kernel,
        grid_spec=pltpu.PrefetchScalarGridSpec(
            num_scalar_prefetch=2,
            grid=(n_blk,),
            in_specs=[pl.BlockSpec((MOE_ROWS, dp), blk),
                      pl.BlockSpec((1, 1, d, 2 * E_HIDDEN), wsel),
                      pl.BlockSpec((1, 1, E_HIDDEN, d), wsel)],
            out_specs=pl.BlockSpec((MOE_ROWS, dp), lambda i, be, nu: (i, 0)),
        ),
        out_shape=jax.ShapeDtypeStruct((n_slots, dp), jnp.uint32),
        compiler_params=_cparams(1),
        name="experts",
    )(blk_e, n_used, xs, w_e_in, w_e_out)


def _combine_kernel(dest_ref, y_ref, h_ref, x1_ref, wn_ref, ml_ref, mc_ref, wsi_ref, wso_ref, ln_ref,
                    x2_ref, g_scr, sem, *, tm, n_lat, n_tok, alpha):
    def start(r, c):
        for k in range(TOP_K):
            pltpu.make_async_copy(y_ref.at[pl.ds(dest_ref[k, r], 1), :], g_scr.at[k, pl.ds(r, 1), :], sem).start()
        return c

    lax.fori_loop(0, tm, start, 0)
    h = h_ref[...]
    ag = _dot(h.astype(BF16), wsi_ref[...])
    a = ag[:, :S_HIDDEN]
    g = ag[:, S_HIDDEN:]
    f = _dot((a * _sigmoid(a) * g).astype(BF16), wso_ref[...])
    wn = wn_ref[...]
    f_hi = f[:, :D_MODEL // 2]
    f_lo = f[:, D_MODEL // 2:]
    for k in range(TOP_K):
        pltpu.make_async_copy(y_ref.at[pl.ds(0, tm), :], g_scr.at[k], sem).wait()
    for k in range(TOP_K):
        y_hi, y_lo = _unpack_bf16(g_scr[k])
        f_hi = f_hi + wn[:, k:k + 1] * y_hi
        f_lo = f_lo + wn[:, k:k + 1] * y_lo
    f = jnp.concatenate([f_hi, f_lo], axis=1)
    rows = (pl.program_id(0) * tm) % n_tok + lax.broadcasted_iota(jnp.int32, (tm, 1), 0)
    is_ctx = rows >= n_lat
    u = alpha * x1_ref[...] + _pick_mod(is_ctx, mc_ref, ml_ref, 5) * f
    x2_ref[...] = _layer_norm(u, ln_ref[0:1, :], ln_ref[1:2, :])


def _combine(dest, y, h2, x1, wn, mod_lat, mod_ctx, ws_in, ws_out, ln2, *, n_lat, n_tok, alpha):
    n, d = h2.shape
    tm = TOK_TILE
    tiles_per_batch = n_tok // tm
    row = lambda i: (i, 0)
    const = lambda i: (0, 0)
    return pl.pallas_call(
        functools.partial(_combine_kernel, tm=tm, n_lat=n_lat, n_tok=n_tok, alpha=alpha),
        grid=(n // tm,),
        in_specs=[pl.BlockSpec((TOP_K, tm), lambda i: (0, i), memory_space=pltpu.SMEM),
                  pl.BlockSpec(memory_space=pl.ANY),
                  pl.BlockSpec((tm, d), row),
                  pl.BlockSpec((tm, d), row),
                  pl.BlockSpec((tm, 128), row),
                  pl.BlockSpec((1, 8, d), lambda i: (i // tiles_per_batch, 0, 0)),
                  pl.BlockSpec((8, d), const),
                  pl.BlockSpec((d, 2 * S_HIDDEN), const),
                  pl.BlockSpec((S_HIDDEN, d), const),
                  pl.BlockSpec((8, d), const)],
        out_specs=pl.BlockSpec((tm, d), row),
        out_shape=jax.ShapeDtypeStruct((n, d), F32),
        scratch_shapes=[pltpu.VMEM((TOP_K, tm, d // 2), jnp.uint32), pltpu.SemaphoreType.DMA(())],
        compiler_params=_cparams(1),
        name="combine",
    )(dest, y, h2, x1, wn, mod_lat, mod_ctx, ws_in, ws_out, ln2)


def _rope_tables(n_lat, n_ctx):
    rows = n_lat // GRID_W
    row = jnp.repeat(jnp.arange(rows, dtype=F32), GRID_W)
    col = jnp.tile(jnp.arange(GRID_W, dtype=F32), rows)
    n_freq = A_DIM // 4
    inv = jnp.power(ROPE_THETA, -jnp.arange(n_freq, dtype=F32) / n_freq)
    ang_r = row[:, None] * inv
    ang_c = col[:, None] * inv
    cos = jnp.concatenate([jnp.cos(ang_r), jnp.cos(ang_r), jnp.cos(ang_c), jnp.cos(ang_c)], axis=1)
    sin = jnp.concatenate([-jnp.sin(ang_r), jnp.sin(ang_r), -jnp.sin(ang_c), jnp.sin(ang_c)], axis=1)
    cos = jnp.concatenate([jnp.tile(cos, (1, 2)), jnp.ones((n_ctx, 128), F32)], axis=0)
    sin = jnp.concatenate([jnp.tile(sin, (1, 2)), jnp.zeros((n_ctx, 128), F32)], axis=0)
    return cos, sin


def _row_tile(n_tok):
    for tm in (1280, 640, 256):
        if n_tok % tm == 0:
            return tm
    raise ValueError(f"unsupported token count {n_tok}")


def _pad_rows(a, rows):
    return jnp.concatenate([a, jnp.zeros((rows - a.shape[0],) + a.shape[1:], a.dtype)], axis=0)


def kernel(x, c, ctx, c_ctx, ada_w, ada_b, w_in, b_gates, conv_qk, lam, subln_a, norm_m, w_br_a, w_br_m, w_out,
           ln1_g, ln1_b, ln2_g, ln2_b, w_router, e_bias, w_e_in, w_e_out, ws_in, ws_out):
    bsz, n_lat, d = x.shape
    n_ctx = ctx.shape[1]
    n_tok = n_lat + n_ctx
    depth = ada_w.shape[0]
    assert d == D_MODEL and n_ctx == MLSTM_CHUNK and n_lat % 512 == 0 and bsz + 1 <= 8
    alpha = (2 * depth) ** 0.25
    tm = _row_tile(n_tok)
    tq = 512
    ck = next(c for c in (1280, 256) if n_tok % c == 0)
    n_all = bsz * n_tok
    n_assign = n_all * TOP_K
    n_blk = -(-n_assign // MOE_ROWS) + N_EXPERTS
    n_slots = n_blk * MOE_ROWS

    cond = _pad_rows(jnp.concatenate([c, c_ctx[None, :]], axis=0), 8)
    mods = _ada(cond, ada_w, ada_b).reshape(depth, 8, N_MOD, d)
    cos_t, sin_t = _rope_tables(n_lat, n_ctx)
    xc = jnp.concatenate([x, ctx], axis=1)
    g_lo = 6 * COL_TILE
    g_hi = g_lo + 4 * M_HEADS

    for l in range(depth):
        lam_init = 0.8 - 0.6 * math.exp(-0.3 * l)
        mod_lat = _pad_rows(mods[l, :bsz].transpose(1, 0, 2), 8).transpose(1, 0, 2)
        mod_ctx = _pad_rows(mods[l, bsz], 8)
        w_main = jnp.concatenate([w_in[l, :, :g_lo], w_in[l, :, g_hi:]], axis=1).astype(BF16)
        w_gates = jnp.pad(w_in[l, :, g_lo:g_hi], ((0, 0), (0, 128 - 4 * M_HEADS)))
        bg = jnp.pad(b_gates[l], (0, 128 - 4 * M_HEADS))[None, :]
        conv_w = _pad_rows(conv_qk[l], 8)
        lam_p = jnp.pad(lam[l], ((0, 4), (0, 128 - A_DIM)))
        subln = subln_a[l][None, :]

        z, gates = _proj(xc, mod_lat, mod_ctx, w_main, w_gates, bg, cos_t, sin_t, conv_w, n_lat=n_lat, tm=tm)
        oa_lat = _attention(z, lam_p, subln, n_lat=n_lat, tq=tq, ck=ck, lat_queries=True, lam_init=lam_init)
        oa_ctx = _attention(z, lam_p, subln, n_lat=n_lat, tq=n_ctx, ck=ck, lat_queries=False, lam_init=lam_init)
        gates_t = gates[:, :, :4 * M_HEADS].transpose(0, 2, 1)
        hf, hb = _mlstm(z, gates, gates_t)
        ln1 = _pad_rows(jnp.stack([ln1_g[l], ln1_b[l]]), 8)
        x1, h2 = _merge(xc, oa_lat, oa_ctx, hf, hb, z, mod_lat, mod_ctx, norm_m[l][None, :], w_br_a[l].astype(BF16),
                        w_br_m[l].astype(BF16), w_out[l].astype(BF16), ln1, n_lat=n_lat, tm=TOK_TILE, alpha=alpha)

        h2f = h2.reshape(n_all, d)
        idx, wn, pos, cnt = _router(h2f, w_router[l], e_bias[l][None, :])
        counts = cnt[0]
        padded = (counts + MOE_ROWS - 1) // MOE_ROWS * MOE_ROWS
        pend = jnp.cumsum(padded)
        pstart = pend - padded
        dest = _slots(idx, pos, jnp.broadcast_to(pstart.astype(F32)[None, :], (8, N_EXPERTS)))
        blk_e = jnp.minimum(jnp.searchsorted(pend, jnp.arange(n_blk, dtype=jnp.int32) * MOE_ROWS, side="right"),
                            N_EXPERTS - 1).astype(jnp.int32)
        n_used = (pend[-1:] // MOE_ROWS).astype(jnp.int32)
        xs = _dispatch(dest, h2f, jnp.zeros((n_slots, d // 2), jnp.uint32))
        y = _experts(blk_e, n_used, xs, w_e_in, w_e_out, l)
        ln2 = _pad_rows(jnp.stack([ln2_g[l], ln2_b[l]]), 8)
        x2 = _combine(dest, y, h2f, x1.reshape(n_all, d), wn, mod_lat, mod_ctx, ws_in[l].astype(BF16),
                      ws_out[l].astype(BF16), ln2, n_lat=n_lat, n_tok=n_tok, alpha=alpha)
        xc = x2.reshape(bsz, n_tok, d)
    return xc[:, :n_lat]
```

```python
import functools
import math

import jax
import jax.numpy as jnp
from jax import lax
from jax.experimental import pallas as pl
from jax.experimental.pallas import tpu as pltpu

F32 = jnp.float32
BF16 = jnp.bfloat16

D_MODEL = 1024
GRID_W = 64
A_HEADS = 8
A_DIM = 64
A_VDIM = 2 * A_DIM
ROPE_THETA = 10000.0
Q_SCALE = A_DIM ** -0.5 * math.log2(math.e)
M_HEADS = 4
M_QK = 128
M_V = 256
M_VAUG = M_V + 128
N_EXPERTS = 256
TOP_K = 8
E_HIDDEN = 256
S_HIDDEN = 256
ROUTE_SCALE = 2.5
LN_EPS = 1e-5
RMS_EPS = 1e-6
N_MOD = 6
COL_TILE = 1024
N_COL_TILES = 8
PROJ_GROUP = 512
ATTN_UNROLL = 4
MLSTM_CHUNK = 256
MOE_ROWS = 256
TOK_TILE = 256
VMEM_LIMIT = 56 * 1024 * 1024


def _cparams(n_axes):
    return pltpu.CompilerParams(dimension_semantics=("arbitrary",) * n_axes,
                                vmem_limit_bytes=VMEM_LIMIT)


def _dot(a, b):
    return jnp.dot(a, b, preferred_element_type=F32)


def _dot_nt(a, b):
    return lax.dot_general(a, b, (((1,), (1,)), ((), ())), preferred_element_type=F32)


def _split_bf16(a):
    hi = a.astype(BF16)
    lo = (a - hi.astype(F32)).astype(BF16)
    return hi, lo


def _dot3(a, b):
    ah, al = _split_bf16(a)
    bh, bl = _split_bf16(b)
    return _dot(ah, bh) + _dot(al, bh) + _dot(ah, bl)


def _pack_bf16(x):
    n = x.shape[1] // 2
    hi = lax.bitcast_convert_type(x[:, :n].astype(BF16).astype(F32), jnp.uint32)
    lo = lax.bitcast_convert_type(x[:, n:].astype(BF16).astype(F32), jnp.uint32)
    return hi | (lo >> 16)


def _unpack_bf16(p):
    hi = lax.bitcast_convert_type(p & jnp.uint32(0xFFFF0000), F32)
    lo = lax.bitcast_convert_type(p << 16, F32)
    return hi, lo


def _sigmoid(v):
    return 1.0 / (1.0 + jnp.exp(-v))


def _log_sigmoid(v):
    return jnp.minimum(v, 0.0) - jnp.log(1.0 + jnp.exp(-jnp.abs(v)))


def _layer_norm(u, g, b):
    mu = jnp.mean(u, axis=-1, keepdims=True)
    var = jnp.mean(jnp.square(u - mu), axis=-1, keepdims=True)
    return (u - mu) * lax.rsqrt(var + LN_EPS) * g + b


def _pick_mod(is_ctx, mc_ref, ml_ref, k):
    return jnp.where(is_ctx, mc_ref[k:k + 1, :], ml_ref[0, k:k + 1, :])


def _ada_kernel(cond_ref, w_ref, b_ref, o_ref):
    c = cond_ref[...]
    c = c * _sigmoid(c)
    o_ref[0] = _dot3(c, w_ref[0]) + b_ref[0]


def _ada(cond, ada_w, ada_b):
    depth, d, width = ada_w.shape
    tn = 512
    return pl.pallas_call(
        _ada_kernel,
        grid=(depth, width // tn),
        in_specs=[pl.BlockSpec((8, d), lambda l, j: (0, 0)),
                  pl.BlockSpec((1, d, tn), lambda l, j: (l, 0, j)),
                  pl.BlockSpec((1, 1, tn), lambda l, j: (l, 0, j))],
        out_specs=pl.BlockSpec((1, 8, tn), lambda l, j: (l, 0, j)),
        out_shape=jax.ShapeDtypeStruct((depth, 8, width), F32),
        compiler_params=_cparams(2),
        name="ada",
    )(cond, ada_w, ada_b.reshape(depth, 1, width))


def _proj_kernel(x_ref, xp_ref, xn_ref, ml_ref, mc_ref, w_ref, wg_ref, bg_ref, cos_ref, sin_ref,
                 conv_ref, z_ref, gates_ref, h_scr, halo_scr, *, tm, n_lat, n_tok):
    i = pl.program_id(1)
    j = pl.program_id(2)
    row0 = i * tm

    def modulate(xv, rows):
        is_ctx = rows >= n_lat
        return xv * (1.0 + _pick_mod(is_ctx, mc_ref, ml_ref, 1)) + _pick_mod(is_ctx, mc_ref, ml_ref, 0)

    @pl.when(j == 0)
    def _():
        rows = row0 + lax.broadcasted_iota(jnp.int32, (tm, 1), 0)
        h = modulate(x_ref[0], rows)
        h_scr[...] = h.astype(BF16)
        r8 = lax.broadcasted_iota(jnp.int32, (8, 1), 0)
        halo_scr[0:8, :] = modulate(xp_ref[0], row0 - 8 + r8)
        halo_scr[8:16, :] = modulate(xn_ref[0], row0 + tm + r8)
        gates_ref[0] = _dot3(h, wg_ref[...]) + bg_ref[...]

    def col_groups(fn):
        for lo in range(0, COL_TILE, PROJ_GROUP):
            fn(_dot(h_scr[...], w_ref[:, lo:lo + PROJ_GROUP]), lo)

    @pl.when(j <= 1)
    def _():
        lane = lax.broadcasted_iota(jnp.int32, (tm, 128), 1)
        first = (lane % 32) < 16
        scale = jnp.where(j == 0, Q_SCALE, 1.0).astype(F32)
        cos = cos_ref[...] * scale
        sin = sin_ref[...] * scale

        def rope(zt, lo):
            for g in range(PROJ_GROUP // 128):
                zg = zt[:, g * 128:(g + 1) * 128]
                partner = jnp.where(first, pltpu.roll(zg, 128 - 16, 1), pltpu.roll(zg, 16, 1))
                z_ref[0, :, lo + g * 128:lo + (g + 1) * 128] = (zg * cos + partner * sin).astype(BF16)

        col_groups(rope)

    @pl.when(j == 3)
    def _():
        loc = lax.broadcasted_iota(jnp.int32, (tm, 1), 0)
        rows = row0 + loc
        seg_start = (rows == 0) | (rows == n_lat)
        seg_end = (rows == n_lat - 1) | (rows == n_tok - 1)
        halo = halo_scr[...].astype(BF16)

        def conv(zt, lo):
            zh = _dot(halo, w_ref[:, lo:lo + PROJ_GROUP])
            zd = jnp.where(loc == 0, zh[7:8, :], pltpu.roll(zt, 1, 0))
            zd = jnp.where(seg_start, 0.0, zd)
            zu = jnp.where(loc == tm - 1, zh[8:9, :], pltpu.roll(zt, tm - 1, 0))
            zu = jnp.where(seg_end, 0.0, zu)
            cw = conv_ref[:, lo:lo + PROJ_GROUP]
            y = zd * cw[0:1, :] + zt * cw[1:2, :] + zu * cw[2:3, :]
            y = y * _sigmoid(y)
            if lo >= M_HEADS * M_QK:
                y = y * (M_QK ** -0.5)
            z_ref[0, :, lo:lo + PROJ_GROUP] = y.astype(BF16)

        col_groups(conv)

    @pl.when((j == 2) | (j >= 4))
    def _():
        def plain(zt, lo):
            z_ref[0, :, lo:lo + PROJ_GROUP] = zt.astype(BF16)

        col_groups(plain)


def _proj(xc, mod_lat, mod_ctx, w_main, w_gates, b_gates, cos_t, sin_t, conv_w, *, n_lat, tm):
    bsz, n_tok, d = xc.shape
    nt8 = n_tok // 8
    kern = functools.partial(_proj_kernel, tm=tm, n_lat=n_lat, n_tok=n_tok)
    return pl.pallas_call(
        kern,
        grid=(bsz, n_tok // tm, N_COL_TILES),
        in_specs=[
            pl.BlockSpec((1, tm, d), lambda b, i, j: (b, i, 0)),
            pl.BlockSpec((1, 8, d), lambda b, i, j: (b, jnp.maximum(i * (tm // 8) - 1, 0), 0)),
            pl.BlockSpec((1, 8, d), lambda b, i, j: (b, jnp.minimum((i + 1) * (tm // 8), nt8 - 1), 0)),
            pl.BlockSpec((1, 8, d), lambda b, i, j: (b, 0, 0)),
            pl.BlockSpec((8, d), lambda b, i, j: (0, 0)),
            pl.BlockSpec((d, COL_TILE), lambda b, i, j: (0, j)),
            pl.BlockSpec((d, 128), lambda b, i, j: (0, 0)),
            pl.BlockSpec((1, 128), lambda b, i, j: (0, 0)),
            pl.BlockSpec((tm, 128), lambda b, i, j: (i, 0)),
            pl.BlockSpec((tm, 128), lambda b, i, j: (i, 0)),
            pl.BlockSpec((8, COL_TILE), lambda b, i, j: (0, 0)),
        ],
        out_specs=[
            pl.BlockSpec((1, tm, COL_TILE), lambda b, i, j: (b, i, j)),
            pl.BlockSpec((1, tm, 128), lambda b, i, j: (b, i, 0)),
        ],
        out_shape=[jax.ShapeDtypeStruct((bsz, n_tok, N_COL_TILES * COL_TILE), BF16),
                   jax.ShapeDtypeStruct((bsz, n_tok, 128), F32)],
        scratch_shapes=[pltpu.VMEM((tm, d), BF16), pltpu.VMEM((16, d), F32)],
        compiler_params=_cparams(3),
        name="proj",
    )(xc, xc, xc, mod_lat, mod_ctx, w_main, w_gates, b_gates, cos_t, sin_t, conv_w)


def _attn_kernel(lam_ref, g_ref, q_ref, k_ref, v_ref, o_ref, *s_scr, tq, ck, n_chunks, kv_lo, lam_init):
    q = q_ref[0]
    lane = lax.broadcasted_iota(jnp.int32, (1, A_VDIM), 1)
    zero = jnp.zeros_like(q)
    qs = (jnp.where(lane < A_DIM, q, zero), jnp.where(lane >= A_DIM, q, zero))
    slots = (s_scr[0:2], s_scr[2:4])

    ones_blk = jnp.where(lax.broadcasted_iota(jnp.int32, (ck, A_VDIM), 1) == 0, 1.0, 0.0).astype(BF16)

    def kv_start(c):
        return pl.multiple_of(kv_lo + c * ck, 128)

    def scores(c, slot):
        k = k_ref[0, pl.ds(kv_start(c), ck), :]
        for qj, s_ref in zip(qs, slots[slot]):
            s_ref[...] = _dot_nt(qj, k)

    def absorb(c, slot, carry):
        v_aug = jnp.concatenate([v_ref[0, pl.ds(kv_start(c), ck), :], ones_blk], axis=1)
        out = []
        for s_ref, (m, acc) in zip(slots[slot], carry):
            s = s_ref[...]
            m_new = jnp.maximum(m, jnp.max(s, axis=-1, keepdims=True))
            alpha = jnp.exp2(m - m_new)
            p = jnp.exp2(s - m_new)
            acc = alpha * acc + _dot(p.astype(BF16), v_aug)
            out.append((m_new, acc))
        return tuple(out)

    def run(c0, count, carry):
        for u in range(count):
            scores(c0 + u + 1, (u + 1) % 2)
            carry = absorb(c0 + u, u % 2, carry)
        return carry

    one = (jnp.full((tq, 1), -jnp.inf, F32), jnp.zeros((tq, 2 * A_VDIM), F32))
    scores(0, 0)
    trips = (n_chunks - 1) // ATTN_UNROLL
    carry = (one, one)
    if trips > 0:
        carry = lax.fori_loop(0, trips, lambda i, cr: run(i * ATTN_UNROLL, ATTN_UNROLL, cr), carry)
    rest = n_chunks - 1 - trips * ATTN_UNROLL
    carry = run(trips * ATTN_UNROLL, rest, carry)
    carry = absorb(n_chunks - 1, rest % 2, carry)

    lam = lam_ref[...]
    s1 = jnp.sum(lam[0:1, :] * lam[1:2, :], axis=-1, keepdims=True)
    s2 = jnp.sum(lam[2:3, :] * lam[3:4, :], axis=-1, keepdims=True)
    lam_val = jnp.exp(s1) - jnp.exp(s2) + lam_init
    (_, acc0), (_, acc1) = carry
    o = (acc0[:, :A_VDIM] / acc0[:, A_VDIM:A_VDIM + 1]
         - lam_val * (acc1[:, :A_VDIM] / acc1[:, A_VDIM:A_VDIM + 1]))
    y = o * lax.rsqrt(jnp.mean(o * o, axis=-1, keepdims=True) + RMS_EPS) * g_ref[...] * (1.0 - lam_init)
    o_ref[0] = y.astype(BF16)


def _attention(z, lam_p, subln, *, n_lat, tq, ck, lat_queries, lam_init):
    bsz, n_tok, _ = z.shape
    n_ctx = n_tok - n_lat
    n_q = n_lat if lat_queries else n_ctx
    qb = 0 if lat_queries else n_lat // tq
    if not lat_queries:
        ck = n_ctx
    kern = functools.partial(_attn_kernel, tq=tq, ck=ck, n_chunks=(n_tok if lat_queries else n_ctx) // ck,
                             kv_lo=0 if lat_queries else n_lat, lam_init=lam_init)
    return pl.pallas_call(
        kern,
        grid=(bsz, A_HEADS, n_q // tq),
        in_specs=[
            pl.BlockSpec((8, 128), lambda b, h, i: (0, 0)),
            pl.BlockSpec((1, 128), lambda b, h, i: (0, 0)),
            pl.BlockSpec((1, tq, A_VDIM), lambda b, h, i: (b, qb + i, h)),
            pl.BlockSpec((1, n_tok, A_VDIM), lambda b, h, i: (b, 0, A_HEADS + h)),
            pl.BlockSpec((1, n_tok, A_VDIM), lambda b, h, i: (b, 0, 2 * A_HEADS + h)),
        ],
        out_specs=pl.BlockSpec((1, tq, A_VDIM), lambda b, h, i: (b, i, h)),
        out_shape=jax.ShapeDtypeStruct((bsz, n_q, A_HEADS * A_VDIM), BF16),
        scratch_shapes=[pltpu.VMEM((tq, ck), F32)] * 4,
        compiler_params=_cparams(3),
        name="attn_lat" if lat_queries else "attn_ctx",
    )(lam_p, subln, z, z, z)


def _mlstm_kernel(qf_ref, kf_ref, vf_ref, gf_ref, gtf_ref, qb_ref, kb_ref, vb_ref, gb_ref, gtb_ref,
                  hf_ref, hb_ref, c_scr, m_scr, *, t):
    step = pl.program_id(1)

    @pl.when(step == 0)
    def _():
        c_scr[...] = jnp.zeros_like(c_scr)
        m_scr[...] = jnp.zeros_like(m_scr)

    ti = lax.broadcasted_iota(jnp.int32, (t, t), 0)
    si = lax.broadcasted_iota(jnp.int32, (t, t), 1)
    lane = lax.broadcasted_iota(jnp.int32, (t, 128), 1)
    ones_blk = jnp.where(lane == 0, 1.0, 0.0).astype(BF16)
    dirs = ((qf_ref, kf_ref, vf_ref, gf_ref, gtf_ref, hf_ref), (qb_ref, kb_ref, vb_ref, gb_ref, gtb_ref, hb_ref))
    for d, (q_ref, k_ref, v_ref, g_ref, gt_ref, h_ref) in enumerate(dirs):
        seen = (si <= ti) if d == 0 else (si >= ti)
        seen_t = (ti <= si) if d == 0 else (ti >= si)
        for hd in range(M_HEADS):
            idx = d * M_HEADS + hd
            q = q_ref[0, :, hd * M_QK:(hd + 1) * M_QK]
            k = k_ref[0, :, hd * M_QK:(hd + 1) * M_QK]
            v = v_ref[0, :, hd * M_V:(hd + 1) * M_V]
            li = d * 2 * M_HEADS + hd
            lf = li + M_HEADS
            i_col = g_ref[0, :, li:li + 1]
            i_row = gt_ref[0, li:li + 1, :]
            lf_col = _log_sigmoid(g_ref[0, :, lf:lf + 1])
            lf_row = _log_sigmoid(gt_ref[0, lf:lf + 1, :])
            b_col = jnp.sum(jnp.where(seen, lf_row, 0.0), axis=1, keepdims=True)
            b_row = jnp.sum(jnp.where(seen_t, lf_col, 0.0), axis=0, keepdims=True)
            b_end = jnp.sum(lf_row, axis=1, keepdims=True)
            m_old = m_scr[idx, 0:1, 0:1]
            dmat = jnp.where(seen, b_col - b_row + i_row, -jnp.inf)
            m_t = jnp.maximum(b_col + m_old, jnp.max(dmat, axis=1, keepdims=True))
            dexp = jnp.exp(dmat - m_t)
            iscale = jnp.exp(b_col + m_old - m_t)
            sc = (_dot_nt(q, k) * dexp).astype(BF16)
            c_aug = c_scr[idx]
            v_aug = jnp.concatenate([v, ones_blk], axis=1)
            inter = _dot(q, c_aug.astype(BF16))
            intra = _dot(sc, v_aug)
            num = iscale * inter[:, :M_V] + intra[:, :M_V]
            den = iscale * inter[:, M_V:M_V + 1] + intra[:, M_V:M_V + 1]
            h_ref[0, :, hd * M_V:(hd + 1) * M_V] = num / jnp.maximum(jnp.abs(den), jnp.exp(-m_t))
            g_col = b_end - b_col + i_col
            m_new = jnp.maximum(b_end + m_old, jnp.max(g_col, axis=0, keepdims=True))
            w_col = jnp.exp(g_col - m_new)
            decay = jnp.exp(b_end + m_old - m_new)
            wv = (w_col * v_aug.astype(F32)).astype(BF16)
            k_t = k.astype(F32).T.astype(BF16)
            c_scr[idx] = decay * c_aug + _dot(k_t, wv)
            m_scr[idx] = jnp.broadcast_to(m_new, (8, 128))


def _mlstm(z, gates, gates_t):
    bsz, n_tok, _ = z.shape
    t = MLSTM_CHUNK
    nc = n_tok // t
    qcol = 3 * COL_TILE // (M_HEADS * M_QK)

    def cf(s):
        return jnp.where(s == 0, nc - 1, s - 1)

    def cb(s):
        return nc - 1 - s

    def specs(c):
        return [
            pl.BlockSpec((1, t, M_HEADS * M_QK), lambda b, s: (b, c(s), qcol)),
            pl.BlockSpec((1, t, M_HEADS * M_QK), lambda b, s: (b, c(s), qcol + 1)),
            pl.BlockSpec((1, t, M_HEADS * M_V), lambda b, s: (b, c(s), 4)),
            pl.BlockSpec((1, t, 128), lambda b, s: (b, c(s), 0)),
            pl.BlockSpec((1, 16, t), lambda b, s: (b, 0, c(s))),
        ]

    out_sd = jax.ShapeDtypeStruct((bsz, n_tok, M_HEADS * M_V), F32)
    return pl.pallas_call(
        functools.partial(_mlstm_kernel, t=t),
        grid=(bsz, nc),
        in_specs=specs(cf) + specs(cb),
        out_specs=[pl.BlockSpec((1, t, M_HEADS * M_V), lambda b, s: (b, cf(s), 0)),
                   pl.BlockSpec((1, t, M_HEADS * M_V), lambda b, s: (b, cb(s), 0))],
        out_shape=[out_sd, out_sd],
        scratch_shapes=[pltpu.VMEM((2 * M_HEADS, M_QK, M_VAUG), F32), pltpu.VMEM((2 * M_HEADS, 8, 128), F32)],
        compiler_params=_cparams(2),
        name="mlstm",
    )(z, z, z, gates, gates_t, z, z, z, gates, gates_t)


def _merge_kernel(x_ref, oal_ref, oac_ref, hf_ref, hb_ref, om_ref, ga_ref, gb_ref, ml_ref, mc_ref, nm_ref,
                  wa_ref, wm_ref, wo_ref, ln_ref, x1_ref, h2_ref, *, tm, n_lat, alpha):
    i = pl.program_id(1)
    rows = i * tm + lax.broadcasted_iota(jnp.int32, (tm, 1), 0)
    is_ctx = rows >= n_lat
    oa = jnp.where(i * tm >= n_lat, oac_ref[0], oal_ref[0])
    hm = hf_ref[0] + hb_ref[0]
    parts = []
    for hd in range(M_HEADS):
        seg = hm[:, hd * M_V:(hd + 1) * M_V]
        parts.append(seg * lax.rsqrt(jnp.mean(seg * seg, axis=-1, keepdims=True) + RMS_EPS))
    hm = jnp.concatenate(parts, axis=1) * nm_ref[...] * _sigmoid(om_ref[0].astype(F32))
    ya = _dot(oa, wa_ref[...])
    ym = _dot(hm.astype(BF16), wm_ref[...])
    y = _sigmoid(ga_ref[0].astype(F32)) * ya + _sigmoid(gb_ref[0].astype(F32)) * ym
    y = _dot(y.astype(BF16), wo_ref[...])
    u = alpha * x_ref[0] + _pick_mod(is_ctx, mc_ref, ml_ref, 2) * y
    x1 = _layer_norm(u, ln_ref[0:1, :], ln_ref[1:2, :])
    x1_ref[0] = x1
    h2_ref[0] = x1 * (1.0 + _pick_mod(is_ctx, mc_ref, ml_ref, 4)) + _pick_mod(is_ctx, mc_ref, ml_ref, 3)


def _merge(xc, oa_lat, oa_ctx, hf, hb, z, mod_lat, mod_ctx, norm_m, w_a, w_m, w_o, ln1, *, n_lat, tm, alpha):
    bsz, n_tok, d = xc.shape
    row = lambda b, i: (b, i, 0)
    const = lambda b, i: (0, 0)
    out_sd = jax.ShapeDtypeStruct((bsz, n_tok, d), F32)
    lat_tiles = n_lat // tm
    return pl.pallas_call(
        functools.partial(_merge_kernel, tm=tm, n_lat=n_lat, alpha=alpha),
        grid=(bsz, n_tok // tm),
        in_specs=[
            pl.BlockSpec((1, tm, d), row),
            pl.BlockSpec((1, tm, d), lambda b, i: (b, jnp.minimum(i, lat_tiles - 1), 0)),
            pl.BlockSpec((1, tm, d), lambda b, i: (b, jnp.maximum(i - lat_tiles, 0), 0)),
            pl.BlockSpec((1, tm, d), row),
            pl.BlockSpec((1, tm, d), row),
            pl.BlockSpec((1, tm, COL_TILE), lambda b, i: (b, i, 5)),
            pl.BlockSpec((1, tm, COL_TILE), lambda b, i: (b, i, 6)),
            pl.BlockSpec((1, tm, COL_TILE), lambda b, i: (b, i, 7)),
            pl.BlockSpec((1, 8, d), lambda b, i: (b, 0, 0)),
            pl.BlockSpec((8, d), const),
            pl.BlockSpec((1, d), const),
            pl.BlockSpec((d, d), const),
            pl.BlockSpec((d, d), const),
            pl.BlockSpec((d, d), const),
            pl.BlockSpec((8, d), const),
        ],
        out_specs=[pl.BlockSpec((1, tm, d), row), pl.BlockSpec((1, tm, d), row)],
        out_shape=[out_sd, out_sd],
        compiler_params=_cparams(2),
        name="merge",
    )(xc, oa_lat, oa_ctx, hf, hb, z, z, z, mod_lat, mod_ctx, norm_m, w_a, w_m, w_o, ln1)


def _router_kernel(h_ref, wr_ref, eb_ref, idx_ref, wn_ref, pos_ref, cnt_ref, carry_scr, *, tm):
    @pl.when(pl.program_id(0) == 0)
    def _():
        carry_scr[...] = jnp.zeros_like(carry_scr)

    scores = _sigmoid(_dot3(h_ref[...], wr_ref[...]))
    biased = scores + eb_ref[...]
    lane = lax.broadcasted_iota(jnp.int32, (tm, N_EXPERTS), 1).astype(F32)
    hots, sels, ws = [], [], []
    for _ in range(TOP_K):
        mx = jnp.max(biased, axis=-1, keepdims=True)
        sel = jnp.min(jnp.where(biased == mx, lane, float(N_EXPERTS)), axis=-1, keepdims=True)
        hot = lane == sel
        hots.append(hot)
        sels.append(sel)
        ws.append(jnp.sum(jnp.where(hot, scores, 0.0), axis=-1, keepdims=True))
        biased = jnp.where(hot, -jnp.inf, biased)
    wsum = ws[0]
    chosen = hots[0]
    for r in range(1, TOP_K):
        wsum = wsum + ws[r]
        chosen = chosen | hots[r]
    chosen_f = jnp.where(chosen, 1.0, 0.0)
    ti = lax.broadcasted_iota(jnp.int32, (tm, tm), 0)
    si = lax.broadcasted_iota(jnp.int32, (tm, tm), 1)
    before = jnp.where(si < ti, 1.0, 0.0).astype(BF16)
    rank = carry_scr[0:1, :] + _dot(before, chosen_f.astype(BF16))
    out_lane = lax.broadcasted_iota(jnp.int32, (tm, 128), 1)
    idx_o = jnp.zeros((tm, 128), jnp.int32)
    pos_o = jnp.zeros((tm, 128), jnp.int32)
    wn_o = jnp.zeros((tm, 128), F32)
    for r in range(TOP_K):
        pos_r = jnp.sum(jnp.where(hots[r], rank, 0.0), axis=-1, keepdims=True).astype(jnp.int32)
        idx_o = jnp.where(out_lane == r, sels[r].astype(jnp.int32), idx_o)
        pos_o = jnp.where(out_lane == r, pos_r, pos_o)
        wn_o = jnp.where(out_lane == r, ws[r] / wsum * ROUTE_SCALE, wn_o)
    idx_ref[...] = idx_o
    pos_ref[...] = pos_o
    wn_ref[...] = wn_o
    total = carry_scr[0:1, :] + jnp.sum(chosen_f, axis=0, keepdims=True)
    carry_scr[...] = jnp.broadcast_to(total, carry_scr.shape)
    cnt_ref[...] = jnp.broadcast_to(total, cnt_ref.shape).astype(jnp.int32)


def _router(h2, w_router, e_bias):
    n, d = h2.shape
    tm = TOK_TILE
    row = lambda i: (i, 0)
    const = lambda i: (0, 0)
    return pl.pallas_call(
        functools.partial(_router_kernel, tm=tm),
        grid=(n // tm,),
        in_specs=[pl.BlockSpec((tm, d), row), pl.BlockSpec((d, N_EXPERTS), const),
                  pl.BlockSpec((1, N_EXPERTS), const)],
        out_specs=[pl.BlockSpec((tm, 128), row), pl.BlockSpec((tm, 128), row), pl.BlockSpec((tm, 128), row),
                   pl.BlockSpec((8, N_EXPERTS), const)],
        out_shape=[jax.ShapeDtypeStruct((n, 128), jnp.int32), jax.ShapeDtypeStruct((n, 128), F32),
                   jax.ShapeDtypeStruct((n, 128), jnp.int32), jax.ShapeDtypeStruct((8, N_EXPERTS), jnp.int32)],
        scratch_shapes=[pltpu.VMEM((8, N_EXPERTS), F32)],
        compiler_params=_cparams(1),
        name="router",
    )(h2, w_router, e_bias)


def _slots_kernel(idx_ref, pos_ref, start_ref, dest_ref, *, tm):
    lane = lax.broadcasted_iota(jnp.int32, (tm, N_EXPERTS), 1)
    out_lane = lax.broadcasted_iota(jnp.int32, (tm, 128), 1)
    idx = idx_ref[...]
    pos = pos_ref[...]
    start = start_ref[0:1, :]
    dest = jnp.zeros((tm, 128), F32)
    for r in range(TOP_K):
        base = jnp.sum(jnp.where(lane == idx[:, r:r + 1], start, 0.0), axis=-1, keepdims=True)
        dest = jnp.where(out_lane == r, base + pos[:, r:r + 1].astype(F32), dest)
    dest_ref[0] = dest.T[0:TOP_K, :].astype(jnp.int32)


def _slots(idx, pos, start):
    n = idx.shape[0]
    tm = TOK_TILE
    return pl.pallas_call(
        functools.partial(_slots_kernel, tm=tm),
        grid=(n // tm,),
        in_specs=[pl.BlockSpec((tm, 128), lambda i: (i, 0)), pl.BlockSpec((tm, 128), lambda i: (i, 0)),
                  pl.BlockSpec((8, N_EXPERTS), lambda i: (0, 0))],
        out_specs=pl.BlockSpec((1, TOP_K, tm), lambda i: (i, 0, 0)),
        out_shape=jax.ShapeDtypeStruct((n // tm, TOP_K, tm), jnp.int32),
        compiler_params=_cparams(1),
        name="slots",
    )(idx, pos, start).reshape(-1)


def _dispatch_kernel(dest_ref, h_ref, xs_in_ref, xs_ref, pk_scr, sem, *, tm):
    del xs_in_ref
    pk_scr[...] = _pack_bf16(h_ref[...])

    def start(r, c):
        for k in range(TOP_K):
            pltpu.make_async_copy(pk_scr.at[pl.ds(r, 1), :], xs_ref.at[pl.ds(dest_ref[k * tm + r], 1), :], sem).start()
        return c

    lax.fori_loop(0, tm, start, 0)
    for k in range(TOP_K):
        pltpu.make_async_copy(pk_scr, xs_ref.at[pl.ds(0, tm), :], sem).wait()


def _dispatch(dest, h2, xs_init):
    n, d = h2.shape
    tm = TOK_TILE
    return pl.pallas_call(
        functools.partial(_dispatch_kernel, tm=tm),
        grid=(n // tm,),
        in_specs=[pl.BlockSpec((TOP_K * tm,), lambda i: (i,), memory_space=pltpu.SMEM),
                  pl.BlockSpec((tm, d), lambda i: (i, 0)),
                  pl.BlockSpec(memory_space=pl.ANY)],
        out_specs=pl.BlockSpec(memory_space=pl.ANY),
        out_shape=jax.ShapeDtypeStruct(xs_init.shape, xs_init.dtype),
        scratch_shapes=[pltpu.VMEM((tm, d // 2), jnp.uint32), pltpu.SemaphoreType.DMA(())],
        input_output_aliases={2: 0},
        compiler_params=_cparams(1),
        name="dispatch",
    )(dest, h2, xs_init)


def _expert_kernel(blk_e_ref, n_used_ref, xs_ref, wi_ref, wo_ref, y_ref, wi_scr, wo_scr):
    i = pl.program_id(0)
    used = i < n_used_ref[0]

    @pl.when(used & ((i == 0) | (blk_e_ref[i] != blk_e_ref[jnp.maximum(i - 1, 0)])))
    def _():
        wi_scr[...] = wi_ref[0, 0].astype(BF16)
        wo_scr[...] = wo_ref[0, 0].astype(BF16)

    @pl.when(used)
    def _():
        x = jnp.concatenate(_unpack_bf16(xs_ref[...]), axis=1).astype(BF16)
        ag = _dot(x, wi_scr[...])
        a = ag[:, :E_HIDDEN]
        g = ag[:, E_HIDDEN:]
        y_ref[...] = _pack_bf16(_dot((a * _sigmoid(a) * g).astype(BF16), wo_scr[...]))

    @pl.when(jnp.logical_not(used))
    def _():
        y_ref[...] = jnp.zeros_like(y_ref)


def _experts(blk_e, n_used, xs, w_e_in, w_e_out, layer):
    n_slots, dp = xs.shape
    d = 2 * dp
    n_blk = n_slots // MOE_ROWS

    def blk(i, be, nu):
        return (jnp.minimum(i, nu[0] - 1), 0)

    def wsel(i, be, nu):
        return (layer, be[jnp.minimum(i, nu[0] - 1)], 0, 0)

    return pl.pallas_call(
        _expert_kernel,
        grid_spec=pltpu.PrefetchScalarGridSpec(
            num_scalar_prefetch=2,
            grid=(n_blk,),
            in_specs=[pl.BlockSpec((MOE_ROWS, dp), blk),
                      pl.BlockSpec((1, 1, d, 2 * E_HIDDEN), wsel),
                      pl.BlockSpec((1, 1, E_HIDDEN, d), wsel)],
            out_specs=pl.BlockSpec((MOE_ROWS, dp), lambda i, be, nu: (i, 0)),
            scratch_shapes=[pltpu.VMEM((d, 2 * E_HIDDEN), BF16), pltpu.VMEM((E_HIDDEN, d), BF16)],
        ),
        out_shape=jax.ShapeDtypeStruct((n_slots, dp), jnp.uint32),
        compiler_params=_cparams(1),
        name="experts",
    )(blk_e, n_used, xs, w_e_in, w_e_out)


def _combine_kernel(dest_ref, y_ref, h_ref, x1_ref, wn_ref, ml_ref, mc_ref, wsi_ref, wso_ref, ln_ref,
                    x2_ref, g_scr, sem, *, tm, n_lat, n_tok, alpha):
    def start(r, c):
        for k in range(TOP_K):
            pltpu.make_async_copy(y_ref.at[pl.ds(dest_ref[k * tm + r], 1), :], g_scr.at[k, pl.ds(r, 1), :], sem).start()
        return c

    lax.fori_loop(0, tm, start, 0)
    h = h_ref[...]
    ag = _dot(h.astype(BF16), wsi_ref[...])
    a = ag[:, :S_HIDDEN]
    g = ag[:, S_HIDDEN:]
    f = _dot((a * _sigmoid(a) * g).astype(BF16), wso_ref[...])
    wn = wn_ref[...]
    f_hi = f[:, :D_MODEL // 2]
    f_lo = f[:, D_MODEL // 2:]
    for k in range(TOP_K):
        pltpu.make_async_copy(y_ref.at[pl.ds(0, tm), :], g_scr.at[k], sem).wait()
    for k in range(TOP_K):
        y_hi, y_lo = _unpack_bf16(g_scr[k])
        f_hi = f_hi + wn[:, k:k + 1] * y_hi
        f_lo = f_lo + wn[:, k:k + 1] * y_lo
    f = jnp.concatenate([f_hi, f_lo], axis=1)
    rows = (pl.program_id(0) * tm) % n_tok + lax.broadcasted_iota(jnp.int32, (tm, 1), 0)
    is_ctx = rows >= n_lat
    u = alpha * x1_ref[...] + _pick_mod(is_ctx, mc_ref, ml_ref, 5) * f
    x2_ref[...] = _layer_norm(u, ln_ref[0:1, :], ln_ref[1:2, :])


def _combine(dest, y, h2, x1, wn, mod_lat, mod_ctx, ws_in, ws_out, ln2, *, n_lat, n_tok, alpha):
    n, d = h2.shape
    tm = TOK_TILE
    tiles_per_batch = n_tok // tm
    row = lambda i: (i, 0)
    const = lambda i: (0, 0)
    return pl.pallas_call(
        functools.partial(_combine_kernel, tm=tm, n_lat=n_lat, n_tok=n_tok, alpha=alpha),
        grid=(n // tm,),
        in_specs=[pl.BlockSpec((TOP_K * tm,), lambda i: (i,), memory_space=pltpu.SMEM),
                  pl.BlockSpec(memory_space=pl.ANY),
                  pl.BlockSpec((tm, d), row),
                  pl.BlockSpec((tm, d), row),
                  pl.BlockSpec((tm, 128), row),
                  pl.BlockSpec((1, 8, d), lambda i: (i // tiles_per_batch, 0, 0)),
                  pl.BlockSpec((8, d), const),
                  pl.BlockSpec((d, 2 * S_HIDDEN), const),
                  pl.BlockSpec((S_HIDDEN, d), const),
                  pl.BlockSpec((8, d), const)],
        out_specs=pl.BlockSpec((tm, d), row),
        out_shape=jax.ShapeDtypeStruct((n, d), F32),
        scratch_shapes=[pltpu.VMEM((TOP_K, tm, d // 2), jnp.uint32), pltpu.SemaphoreType.DMA(())],
        compiler_params=_cparams(1),
        name="combine",
    )(dest, y, h2, x1, wn, mod_lat, mod_ctx, ws_in, ws_out, ln2)


def _rope_tables(n_lat, n_ctx):
    rows = n_lat // GRID_W
    row = jnp.repeat(jnp.arange(rows, dtype=F32), GRID_W)
    col = jnp.tile(jnp.arange(GRID_W, dtype=F32), rows)
    n_freq = A_DIM // 4
    inv = jnp.power(ROPE_THETA, -jnp.arange(n_freq, dtype=F32) / n_freq)
    ang_r = row[:, None] * inv
    ang_c = col[:, None] * inv
    cos = jnp.concatenate([jnp.cos(ang_r), jnp.cos(ang_r), jnp.cos(ang_c), jnp.cos(ang_c)], axis=1)
    sin = jnp.concatenate([-jnp.sin(ang_r), jnp.sin(ang_r), -jnp.sin(ang_c), jnp.sin(ang_c)], axis=1)
    cos = jnp.concatenate([jnp.tile(cos, (1, 2)), jnp.ones((n_ctx, 128), F32)], axis=0)
    sin = jnp.concatenate([jnp.tile(sin, (1, 2)), jnp.zeros((n_ctx, 128), F32)], axis=0)
    return cos, sin


def _row_tile(n_tok):
    for tm in (1280, 640, 256):
        if n_tok % tm == 0:
            return tm
    raise ValueError(f"unsupported token count {n_tok}")


def _pad_rows(a, rows):
    return jnp.concatenate([a, jnp.zeros((rows - a.shape[0],) + a.shape[1:], a.dtype)], axis=0)


def kernel(x, c, ctx, c_ctx, ada_w, ada_b, w_in, b_gates, conv_qk, lam, subln_a, norm_m, w_br_a, w_br_m, w_out,
           ln1_g, ln1_b, ln2_g, ln2_b, w_router, e_bias, w_e_in, w_e_out, ws_in, ws_out):
    bsz, n_lat, d = x.shape
    n_ctx = ctx.shape[1]
    n_tok = n_lat + n_ctx
    depth = ada_w.shape[0]
    assert d == D_MODEL and n_ctx == MLSTM_CHUNK and n_lat % 512 == 0 and bsz + 1 <= 8
    alpha = (2 * depth) ** 0.25
    tm = _row_tile(n_tok)
    tq = 512
    ck = next(c for c in (1280, 256) if n_tok % c == 0)
    n_all = bsz * n_tok
    n_assign = n_all * TOP_K
    n_blk = -(-n_assign // MOE_ROWS) + N_EXPERTS
    n_slots = n_blk * MOE_ROWS

    cond = _pad_rows(jnp.concatenate([c, c_ctx[None, :]], axis=0), 8)
    mods = _ada(cond, ada_w, ada_b).reshape(depth, 8, N_MOD, d)
    cos_t, sin_t = _rope_tables(n_lat, n_ctx)
    xc = jnp.concatenate([x, ctx], axis=1)
    g_lo = 6 * COL_TILE
    g_hi = g_lo + 4 * M_HEADS

    for l in range(depth):
        lam_init = 0.8 - 0.6 * math.exp(-0.3 * l)
        mod_lat = _pad_rows(mods[l, :bsz].transpose(1, 0, 2), 8).transpose(1, 0, 2)
        mod_ctx = _pad_rows(mods[l, bsz], 8)
        w_main = jnp.concatenate([w_in[l, :, :g_lo], w_in[l, :, g_hi:]], axis=1).astype(BF16)
        w_gates = jnp.pad(w_in[l, :, g_lo:g_hi], ((0, 0), (0, 128 - 4 * M_HEADS)))
        bg = jnp.pad(b_gates[l], (0, 128 - 4 * M_HEADS))[None, :]
        conv_w = _pad_rows(conv_qk[l], 8)
        lam_p = jnp.pad(lam[l], ((0, 4), (0, 128 - A_DIM)))
        subln = subln_a[l][None, :]

        z, gates = _proj(xc, mod_lat, mod_ctx, w_main, w_gates, bg, cos_t, sin_t, conv_w, n_lat=n_lat, tm=tm)
        oa_lat = _attention(z, lam_p, subln, n_lat=n_lat, tq=tq, ck=ck, lat_queries=True, lam_init=lam_init)
        oa_ctx = _attention(z, lam_p, subln, n_lat=n_lat, tq=n_ctx, ck=ck, lat_queries=False, lam_init=lam_init)
        gates_t = gates[:, :, :4 * M_HEADS].transpose(0, 2, 1)
        hf, hb = _mlstm(z, gates, gates_t)
        ln1 = _pad_rows(jnp.stack([ln1_g[l], ln1_b[l]]), 8)
        x1, h2 = _merge(xc, oa_lat, oa_ctx, hf, hb, z, mod_lat, mod_ctx, norm_m[l][None, :], w_br_a[l].astype(BF16),
                        w_br_m[l].astype(BF16), w_out[l].astype(BF16), ln1, n_lat=n_lat, tm=TOK_TILE, alpha=alpha)

        h2f = h2.reshape(n_all, d)
        idx, wn, pos, cnt = _router(h2f, w_router[l], e_bias[l][None, :])
        counts = cnt[0]
        padded = (counts + MOE_ROWS - 1) // MOE_ROWS * MOE_ROWS
        pend = jnp.cumsum(padded)
        pstart = pend - padded
        dest = _slots(idx, pos, jnp.broadcast_to(pstart.astype(F32)[None, :], (8, N_EXPERTS)))
        blk_row = jnp.arange(n_blk, dtype=jnp.int32) * MOE_ROWS
        blk_e = jnp.minimum(jnp.sum(pend[None, :] <= blk_row[:, None], axis=1), N_EXPERTS - 1).astype(jnp.int32)
        n_used = (pend[-1:] // MOE_ROWS).astype(jnp.int32)
        xs = _dispatch(dest, h2f, jnp.zeros((n_slots, d // 2), jnp.uint32))
        y = _experts(blk_e, n_used, xs, w_e_in, w_e_out, l)
        ln2 = _pad_rows(jnp.stack([ln2_g[l], ln2_b[l]]), 8)
        x2 = _combine(dest, y, h2f, x1.reshape(n_all, d), wn, mod_lat, mod_ctx, ws_in[l].astype(BF16),
                      ws_out[l].astype(BF16), ln2, n_lat=n_lat, n_tok=n_tok, alpha=alpha)
        xc = x2.reshape(bsz, n_tok, d)
    return xc[:, :n_lat]
```

```python
import functools
import math

import jax
import jax.numpy as jnp
from jax import lax
from jax.experimental import pallas as pl
from jax.experimental.pallas import tpu as pltpu

F32 = jnp.float32
BF16 = jnp.bfloat16

D_MODEL = 1024
GRID_W = 64
A_HEADS = 8
A_DIM = 64
A_VDIM = 2 * A_DIM
ROPE_THETA = 10000.0
Q_SCALE = A_DIM ** -0.5 * math.log2(math.e)
M_HEADS = 4
M_QK = 128
M_V = 256
M_VAUG = M_V + 128
N_EXPERTS = 256
TOP_K = 8
E_HIDDEN = 256
S_HIDDEN = 256
ROUTE_SCALE = 2.5
LN_EPS = 1e-5
RMS_EPS = 1e-6
N_MOD = 6
COL_TILE = 1024
N_COL_TILES = 8
PROJ_GROUP = 512
ATTN_UNROLL = 4
ATTN_TILES = 1
MLSTM_CHUNK = 256
MOE_ROWS = 512
TOK_TILE = 256
VMEM_LIMIT = 56 * 1024 * 1024


def _cparams(n_axes):
    return pltpu.CompilerParams(dimension_semantics=("arbitrary",) * n_axes,
                                vmem_limit_bytes=VMEM_LIMIT)


def _dot(a, b):
    return jnp.dot(a, b, preferred_element_type=F32)


def _dot_nt(a, b):
    return lax.dot_general(a, b, (((1,), (1,)), ((), ())), preferred_element_type=F32)


def _split_bf16(a):
    hi = a.astype(BF16)
    lo = (a - hi.astype(F32)).astype(BF16)
    return hi, lo


def _dot3(a, b):
    ah, al = _split_bf16(a)
    bh, bl = _split_bf16(b)
    return _dot(ah, bh) + _dot(al, bh) + _dot(ah, bl)


def _pack_bf16(x):
    n = x.shape[1] // 2
    hi = lax.bitcast_convert_type(x[:, :n].astype(BF16).astype(F32), jnp.uint32)
    lo = lax.bitcast_convert_type(x[:, n:].astype(BF16).astype(F32), jnp.uint32)
    return hi | (lo >> 16)


def _unpack_bf16(p):
    hi = lax.bitcast_convert_type(p & jnp.uint32(0xFFFF0000), F32)
    lo = lax.bitcast_convert_type(p << 16, F32)
    return hi, lo


def _sigmoid(v):
    return 1.0 / (1.0 + jnp.exp(-v))


def _log_sigmoid(v):
    return jnp.minimum(v, 0.0) - jnp.log(1.0 + jnp.exp(-jnp.abs(v)))


def _layer_norm(u, g, b):
    mu = jnp.mean(u, axis=-1, keepdims=True)
    var = jnp.mean(jnp.square(u - mu), axis=-1, keepdims=True)
    return (u - mu) * lax.rsqrt(var + LN_EPS) * g + b


def _pick_mod(is_ctx, mc_ref, ml_ref, k):
    return jnp.where(is_ctx, mc_ref[k:k + 1, :], ml_ref[0, k:k + 1, :])


def _ada_kernel(cond_ref, w_ref, b_ref, o_ref):
    c = cond_ref[...]
    c = c * _sigmoid(c)
    o_ref[0] = _dot3(c, w_ref[0]) + b_ref[0]


def _ada(cond, ada_w, ada_b):
    depth, d, width = ada_w.shape
    tn = 512
    return pl.pallas_call(
        _ada_kernel,
        grid=(depth, width // tn),
        in_specs=[pl.BlockSpec((8, d), lambda l, j: (0, 0)),
                  pl.BlockSpec((1, d, tn), lambda l, j: (l, 0, j)),
                  pl.BlockSpec((1, 1, tn), lambda l, j: (l, 0, j))],
        out_specs=pl.BlockSpec((1, 8, tn), lambda l, j: (l, 0, j)),
        out_shape=jax.ShapeDtypeStruct((depth, 8, width), F32),
        compiler_params=_cparams(2),
        name="ada",
    )(cond, ada_w, ada_b.reshape(depth, 1, width))


def _proj_kernel(x_ref, xp_ref, xn_ref, ml_ref, mc_ref, w_ref, wg_ref, bg_ref, cos_ref, sin_ref,
                 conv_ref, z_ref, gates_ref, h_scr, halo_scr, *, tm, n_lat, n_tok):
    i = pl.program_id(1)
    j = pl.program_id(2)
    row0 = i * tm

    def modulate(xv, rows):
        is_ctx = rows >= n_lat
        return xv * (1.0 + _pick_mod(is_ctx, mc_ref, ml_ref, 1)) + _pick_mod(is_ctx, mc_ref, ml_ref, 0)

    @pl.when(j == 0)
    def _():
        rows = row0 + lax.broadcasted_iota(jnp.int32, (tm, 1), 0)
        h = modulate(x_ref[0], rows)
        h_scr[...] = h.astype(BF16)
        r8 = lax.broadcasted_iota(jnp.int32, (8, 1), 0)
        halo_scr[0:8, :] = modulate(xp_ref[0], row0 - 8 + r8)
        halo_scr[8:16, :] = modulate(xn_ref[0], row0 + tm + r8)
        gates_ref[0] = _dot3(h, wg_ref[...]) + bg_ref[...]

    def col_groups(fn):
        for lo in range(0, COL_TILE, PROJ_GROUP):
            fn(_dot(h_scr[...], w_ref[:, lo:lo + PROJ_GROUP]), lo)

    @pl.when(j <= 1)
    def _():
        lane = lax.broadcasted_iota(jnp.int32, (tm, 128), 1)
        first = (lane % 32) < 16
        scale = jnp.where(j == 0, Q_SCALE, 1.0).astype(F32)
        cos = cos_ref[...] * scale
        sin = sin_ref[...] * scale

        def rope(zt, lo):
            for g in range(PROJ_GROUP // 128):
                zg = zt[:, g * 128:(g + 1) * 128]
                partner = jnp.where(first, pltpu.roll(zg, 128 - 16, 1), pltpu.roll(zg, 16, 1))
                z_ref[0, :, lo + g * 128:lo + (g + 1) * 128] = (zg * cos + partner * sin).astype(BF16)

        col_groups(rope)

    @pl.when(j == 3)
    def _():
        loc = lax.broadcasted_iota(jnp.int32, (tm, 1), 0)
        rows = row0 + loc
        seg_start = (rows == 0) | (rows == n_lat)
        seg_end = (rows == n_lat - 1) | (rows == n_tok - 1)
        halo = halo_scr[...].astype(BF16)

        def conv(zt, lo):
            zh = _dot(halo, w_ref[:, lo:lo + PROJ_GROUP])
            zd = jnp.where(loc == 0, zh[7:8, :], pltpu.roll(zt, 1, 0))
            zd = jnp.where(seg_start, 0.0, zd)
            zu = jnp.where(loc == tm - 1, zh[8:9, :], pltpu.roll(zt, tm - 1, 0))
            zu = jnp.where(seg_end, 0.0, zu)
            cw = conv_ref[:, lo:lo + PROJ_GROUP]
            y = zd * cw[0:1, :] + zt * cw[1:2, :] + zu * cw[2:3, :]
            y = y * _sigmoid(y)
            if lo >= M_HEADS * M_QK:
                y = y * (M_QK ** -0.5)
            z_ref[0, :, lo:lo + PROJ_GROUP] = y.astype(BF16)

        col_groups(conv)

    @pl.when((j == 2) | (j >= 4))
    def _():
        def plain(zt, lo):
            z_ref[0, :, lo:lo + PROJ_GROUP] = zt.astype(BF16)

        col_groups(plain)


def _proj(xc, mod_lat, mod_ctx, w_main, w_gates, b_gates, cos_t, sin_t, conv_w, *, n_lat, tm):
    bsz, n_tok, d = xc.shape
    nt8 = n_tok // 8
    kern = functools.partial(_proj_kernel, tm=tm, n_lat=n_lat, n_tok=n_tok)
    return pl.pallas_call(
        kern,
        grid=(bsz, n_tok // tm, N_COL_TILES),
        in_specs=[
            pl.BlockSpec((1, tm, d), lambda b, i, j: (b, i, 0)),
            pl.BlockSpec((1, 8, d), lambda b, i, j: (b, jnp.maximum(i * (tm // 8) - 1, 0), 0)),
            pl.BlockSpec((1, 8, d), lambda b, i, j: (b, jnp.minimum((i + 1) * (tm // 8), nt8 - 1), 0)),
            pl.BlockSpec((1, 8, d), lambda b, i, j: (b, 0, 0)),
            pl.BlockSpec((8, d), lambda b, i, j: (0, 0)),
            pl.BlockSpec((d, COL_TILE), lambda b, i, j: (0, j)),
            pl.BlockSpec((d, 128), lambda b, i, j: (0, 0)),
            pl.BlockSpec((1, 128), lambda b, i, j: (0, 0)),
            pl.BlockSpec((tm, 128), lambda b, i, j: (i, 0)),
            pl.BlockSpec((tm, 128), lambda b, i, j: (i, 0)),
            pl.BlockSpec((8, COL_TILE), lambda b, i, j: (0, 0)),
        ],
        out_specs=[
            pl.BlockSpec((1, tm, COL_TILE), lambda b, i, j: (b, i, j)),
            pl.BlockSpec((1, tm, 128), lambda b, i, j: (b, i, 0)),
        ],
        out_shape=[jax.ShapeDtypeStruct((bsz, n_tok, N_COL_TILES * COL_TILE), BF16),
                   jax.ShapeDtypeStruct((bsz, n_tok, 128), F32)],
        scratch_shapes=[pltpu.VMEM((tm, d), BF16), pltpu.VMEM((16, d), F32)],
        compiler_params=_cparams(3),
        name="proj",
    )(xc, xc, xc, mod_lat, mod_ctx, w_main, w_gates, b_gates, cos_t, sin_t, conv_w)


def _attn_kernel(lam_ref, g_ref, q_ref, k_ref, v_ref, o_ref, *s_scr, tq, tiles, ck, n_chunks, kv_lo, lam_init):
    lane = lax.broadcasted_iota(jnp.int32, (1, A_VDIM), 1)
    slots = (s_scr[0:2], s_scr[2:4])

    def masked_q(t):
        q = q_ref[0, t * tq:(t + 1) * tq, :]
        zero = jnp.zeros_like(q)
        return jnp.where(lane < A_DIM, q, zero), jnp.where(lane >= A_DIM, q, zero)

    ones_blk = jnp.where(lax.broadcasted_iota(jnp.int32, (ck, A_VDIM), 1) == 0, 1.0, 0.0).astype(BF16)

    def kv_start(c):
        return pl.multiple_of(kv_lo + c * ck, 128)

    def scores(qs, c, slot):
        k = k_ref[0, pl.ds(kv_start(c), ck), :]
        for qj, s_ref in zip(qs, slots[slot]):
            s_ref[...] = _dot_nt(qj, k)

    def absorb(c, slot, carry):
        v_aug = jnp.concatenate([v_ref[0, pl.ds(kv_start(c), ck), :], ones_blk], axis=1)
        out = []
        for s_ref, (m, acc) in zip(slots[slot], carry):
            s = s_ref[...]
            m_new = jnp.maximum(m, jnp.max(s, axis=-1, keepdims=True))
            alpha = jnp.exp2(m - m_new)
            p = jnp.exp2(s - m_new)
            acc = alpha * acc + _dot(p.astype(BF16), v_aug)
            out.append((m_new, acc))
        return tuple(out)

    lam = lam_ref[...]
    s1 = jnp.sum(lam[0:1, :] * lam[1:2, :], axis=-1, keepdims=True)
    s2 = jnp.sum(lam[2:3, :] * lam[3:4, :], axis=-1, keepdims=True)
    lam_val = jnp.exp(s1) - jnp.exp(s2) + lam_init

    def finalize(t, carry):
        (_, acc0), (_, acc1) = carry
        o = (acc0[:, :A_VDIM] / acc0[:, A_VDIM:A_VDIM + 1]
             - lam_val * (acc1[:, :A_VDIM] / acc1[:, A_VDIM:A_VDIM + 1]))
        y = o * lax.rsqrt(jnp.mean(o * o, axis=-1, keepdims=True) + RMS_EPS) * g_ref[...] * (1.0 - lam_init)
        o_ref[0, t * tq:(t + 1) * tq, :] = y.astype(BF16)

    one = (jnp.full((tq, 1), -jnp.inf, F32), jnp.zeros((tq, 2 * A_VDIM), F32))
    qs = masked_q(0)
    scores(qs, 0, 0)
    for t in range(tiles):
        base = t * n_chunks

        def run(c0, count, carry, qs=qs, base=base):
            for u in range(count):
                scores(qs, c0 + u + 1, (base + u + 1) % 2)
                carry = absorb(c0 + u, (base + u) % 2, carry)
            return carry

        trips = (n_chunks - 1) // ATTN_UNROLL
        carry = (one, one)
        if trips > 0:
            carry = lax.fori_loop(0, trips, lambda i, cr, run=run: run(i * ATTN_UNROLL, ATTN_UNROLL, cr), carry)
        rest = n_chunks - 1 - trips * ATTN_UNROLL
        carry = run(trips * ATTN_UNROLL, rest, carry)
        if t + 1 < tiles:
            qs = masked_q(t + 1)
            scores(qs, 0, (base + n_chunks) % 2)
        carry = absorb(n_chunks - 1, (base + n_chunks - 1) % 2, carry)
        finalize(t, carry)


def _attention(z, lam_p, subln, *, n_lat, tq, ck, lat_queries, lam_init):
    bsz, n_tok, _ = z.shape
    n_ctx = n_tok - n_lat
    n_q = n_lat if lat_queries else n_ctx
    if not lat_queries:
        ck = n_ctx
    tiles = ATTN_TILES if n_q % (tq * ATTN_TILES) == 0 else 1
    rows = tq * tiles
    qb = 0 if lat_queries else n_lat // rows
    kern = functools.partial(_attn_kernel, tq=tq, tiles=tiles, ck=ck, n_chunks=(n_tok if lat_queries else n_ctx) // ck,
                             kv_lo=0 if lat_queries else n_lat, lam_init=lam_init)
    return pl.pallas_call(
        kern,
        grid=(bsz, A_HEADS, n_q // rows),
        in_specs=[
            pl.BlockSpec((8, 128), lambda b, h, i: (0, 0)),
            pl.BlockSpec((1, 128), lambda b, h, i: (0, 0)),
            pl.BlockSpec((1, rows, A_VDIM), lambda b, h, i: (b, qb + i, h)),
            pl.BlockSpec((1, n_tok, A_VDIM), lambda b, h, i: (b, 0, A_HEADS + h)),
            pl.BlockSpec((1, n_tok, A_VDIM), lambda b, h, i: (b, 0, 2 * A_HEADS + h)),
        ],
        out_specs=pl.BlockSpec((1, rows, A_VDIM), lambda b, h, i: (b, i, h)),
        out_shape=jax.ShapeDtypeStruct((bsz, n_q, A_HEADS * A_VDIM), BF16),
        scratch_shapes=[pltpu.VMEM((tq, ck), F32)] * 4,
        compiler_params=_cparams(3),
        name="attn_lat" if lat_queries else "attn_ctx",
    )(lam_p, subln, z, z, z)


def _mlstm_kernel(qf_ref, kf_ref, vf_ref, gf_ref, gtf_ref, qb_ref, kb_ref, vb_ref, gb_ref, gtb_ref,
                  hf_ref, hb_ref, c_scr, m_scr, *, t):
    step = pl.program_id(1)

    @pl.when(step == 0)
    def _():
        c_scr[...] = jnp.zeros_like(c_scr)
        m_scr[...] = jnp.zeros_like(m_scr)

    ti = lax.broadcasted_iota(jnp.int32, (t, t), 0)
    si = lax.broadcasted_iota(jnp.int32, (t, t), 1)
    lane = lax.broadcasted_iota(jnp.int32, (t, 128), 1)
    ones_blk = jnp.where(lane == 0, 1.0, 0.0).astype(BF16)
    dirs = ((qf_ref, kf_ref, vf_ref, gf_ref, gtf_ref, hf_ref), (qb_ref, kb_ref, vb_ref, gb_ref, gtb_ref, hb_ref))
    units = []
    for d, (q_ref, k_ref, v_ref, g_ref, gt_ref, h_ref) in enumerate(dirs):
        seen = (si <= ti) if d == 0 else (si >= ti)
        seen_t = (ti <= si) if d == 0 else (ti >= si)
        lf_cols = _log_sigmoid(g_ref[0])
        lf_rows = _log_sigmoid(gt_ref[0])
        for hd in range(M_HEADS):
            u = dict(idx=d * M_HEADS + hd, h_ref=h_ref, hd=hd)
            u["q"] = q_ref[0, :, hd * M_QK:(hd + 1) * M_QK]
            u["k"] = k_ref[0, :, hd * M_QK:(hd + 1) * M_QK]
            v = v_ref[0, :, hd * M_V:(hd + 1) * M_V]
            u["v_aug"] = jnp.concatenate([v, ones_blk], axis=1)
            li = d * 2 * M_HEADS + hd
            lf = li + M_HEADS
            i_col = g_ref[0, :, li:li + 1]
            i_row = gt_ref[0, li:li + 1, :]
            lf_col = lf_cols[:, lf:lf + 1]
            lf_row = lf_rows[lf:lf + 1, :]
            b_col = jnp.sum(jnp.where(seen, lf_row, 0.0), axis=1, keepdims=True)
            b_row = jnp.sum(jnp.where(seen_t, lf_col, 0.0), axis=0, keepdims=True)
            b_end = jnp.sum(lf_row, axis=1, keepdims=True)
            m_old = m_scr[u["idx"], 0:1, 0:1]
            dmat = jnp.where(seen, b_col - b_row + i_row, -jnp.inf)
            u["m_t"] = jnp.maximum(b_col + m_old, jnp.max(dmat, axis=1, keepdims=True))
            u["dexp"] = jnp.exp(dmat - u["m_t"])
            u["iscale"] = jnp.exp(b_col + m_old - u["m_t"])
            g_col = b_end - b_col + i_col
            u["m_new"] = jnp.maximum(b_end + m_old, jnp.max(g_col, axis=0, keepdims=True))
            u["w_col"] = jnp.exp(g_col - u["m_new"])
            u["decay"] = jnp.exp(b_end + m_old - u["m_new"])
            units.append(u)
    for u in units:
        u["qk"] = _dot_nt(u["q"], u["k"])
        u["c_aug"] = c_scr[u["idx"]]
        u["inter"] = _dot(u["q"], u["c_aug"].astype(BF16))
    for u in units:
        wv = (u["w_col"] * u["v_aug"].astype(F32)).astype(BF16)
        k_t = u["k"].astype(F32).T.astype(BF16)
        c_scr[u["idx"]] = u["decay"] * u["c_aug"] + _dot(k_t, wv)
        m_scr[u["idx"]] = jnp.broadcast_to(u["m_new"], (8, 128))
    for u in units:
        intra = _dot((u["qk"] * u["dexp"]).astype(BF16), u["v_aug"])
        num = u["iscale"] * u["inter"][:, :M_V] + intra[:, :M_V]
        den = u["iscale"] * u["inter"][:, M_V:M_V + 1] + intra[:, M_V:M_V + 1]
        hd = u["hd"]
        u["h_ref"][0, :, hd * M_V:(hd + 1) * M_V] = num / jnp.maximum(jnp.abs(den), jnp.exp(-u["m_t"]))


def _mlstm(z, gates, gates_t):
    bsz, n_tok, _ = z.shape
    t = MLSTM_CHUNK
    nc = n_tok // t
    qcol = 3 * COL_TILE // (M_HEADS * M_QK)

    def cf(s):
        return jnp.where(s == 0, nc - 1, s - 1)

    def cb(s):
        return nc - 1 - s

    def specs(c):
        return [
            pl.BlockSpec((1, t, M_HEADS * M_QK), lambda b, s: (b, c(s), qcol)),
            pl.BlockSpec((1, t, M_HEADS * M_QK), lambda b, s: (b, c(s), qcol + 1)),
            pl.BlockSpec((1, t, M_HEADS * M_V), lambda b, s: (b, c(s), 4)),
            pl.BlockSpec((1, t, 128), lambda b, s: (b, c(s), 0)),
            pl.BlockSpec((1, 16, t), lambda b, s: (b, 0, c(s))),
        ]

    out_sd = jax.ShapeDtypeStruct((bsz, n_tok, M_HEADS * M_V), F32)
    return pl.pallas_call(
        functools.partial(_mlstm_kernel, t=t),
        grid=(bsz, nc),
        in_specs=specs(cf) + specs(cb),
        out_specs=[pl.BlockSpec((1, t, M_HEADS * M_V), lambda b, s: (b, cf(s), 0)),
                   pl.BlockSpec((1, t, M_HEADS * M_V), lambda b, s: (b, cb(s), 0))],
        out_shape=[out_sd, out_sd],
        scratch_shapes=[pltpu.VMEM((2 * M_HEADS, M_QK, M_VAUG), F32), pltpu.VMEM((2 * M_HEADS, 8, 128), F32)],
        compiler_params=_cparams(2),
        name="mlstm",
    )(z, z, z, gates, gates_t, z, z, z, gates, gates_t)


def _merge_kernel(x_ref, oal_ref, oac_ref, hf_ref, hb_ref, om_ref, ga_ref, gb_ref, ml_ref, mc_ref, nm_ref,
                  wa_ref, wm_ref, wo_ref, ln_ref, x1_ref, h2_ref, *, tm, n_lat, alpha):
    i = pl.program_id(1)
    rows = i * tm + lax.broadcasted_iota(jnp.int32, (tm, 1), 0)
    is_ctx = rows >= n_lat
    oa = jnp.where(i * tm >= n_lat, oac_ref[0], oal_ref[0])
    hm = hf_ref[0] + hb_ref[0]
    parts = []
    for hd in range(M_HEADS):
        seg = hm[:, hd * M_V:(hd + 1) * M_V]
        parts.append(seg * lax.rsqrt(jnp.mean(seg * seg, axis=-1, keepdims=True) + RMS_EPS))
    hm = jnp.concatenate(parts, axis=1) * nm_ref[...] * _sigmoid(om_ref[0].astype(F32))
    ya = _dot(oa, wa_ref[...])
    ym = _dot(hm.astype(BF16), wm_ref[...])
    y = _sigmoid(ga_ref[0].astype(F32)) * ya + _sigmoid(gb_ref[0].astype(F32)) * ym
    y = _dot(y.astype(BF16), wo_ref[...])
    u = alpha * x_ref[0] + _pick_mod(is_ctx, mc_ref, ml_ref, 2) * y
    x1 = _layer_norm(u, ln_ref[0:1, :], ln_ref[1:2, :])
    x1_ref[0] = x1
    h2_ref[0] = x1 * (1.0 + _pick_mod(is_ctx, mc_ref, ml_ref, 4)) + _pick_mod(is_ctx, mc_ref, ml_ref, 3)


def _merge(xc, oa_lat, oa_ctx, hf, hb, z, mod_lat, mod_ctx, norm_m, w_a, w_m, w_o, ln1, *, n_lat, tm, alpha):
    bsz, n_tok, d = xc.shape
    row = lambda b, i: (b, i, 0)
    const = lambda b, i: (0, 0)
    out_sd = jax.ShapeDtypeStruct((bsz, n_tok, d), F32)
    lat_tiles = n_lat // tm
    return pl.pallas_call(
        functools.partial(_merge_kernel, tm=tm, n_lat=n_lat, alpha=alpha),
        grid=(bsz, n_tok // tm),
        in_specs=[
            pl.BlockSpec((1, tm, d), row),
            pl.BlockSpec((1, tm, d), lambda b, i: (b, jnp.minimum(i, lat_tiles - 1), 0)),
            pl.BlockSpec((1, tm, d), lambda b, i: (b, jnp.maximum(i - lat_tiles, 0), 0)),
            pl.BlockSpec((1, tm, d), row),
            pl.BlockSpec((1, tm, d), row),
            pl.BlockSpec((1, tm, COL_TILE), lambda b, i: (b, i, 5)),
            pl.BlockSpec((1, tm, COL_TILE), lambda b, i: (b, i, 6)),
            pl.BlockSpec((1, tm, COL_TILE), lambda b, i: (b, i, 7)),
            pl.BlockSpec((1, 8, d), lambda b, i: (b, 0, 0)),
            pl.BlockSpec((8, d), const),
            pl.BlockSpec((1, d), const),
            pl.BlockSpec((d, d), const),
            pl.BlockSpec((d, d), const),
            pl.BlockSpec((d, d), const),
            pl.BlockSpec((8, d), const),
        ],
        out_specs=[pl.BlockSpec((1, tm, d), row), pl.BlockSpec((1, tm, d), row)],
        out_shape=[out_sd, out_sd],
        compiler_params=_cparams(2),
        name="merge",
    )(xc, oa_lat, oa_ctx, hf, hb, z, z, z, mod_lat, mod_ctx, norm_m, w_a, w_m, w_o, ln1)


def _router_kernel(h_ref, wr_ref, eb_ref, idx_ref, wn_ref, pos_ref, cnt_ref, carry_scr, *, tm):
    @pl.when(pl.program_id(0) == 0)
    def _():
        carry_scr[...] = jnp.zeros_like(carry_scr)

    scores = _sigmoid(_dot3(h_ref[...], wr_ref[...]))
    biased = scores + eb_ref[...]
    lane = lax.broadcasted_iota(jnp.int32, (tm, N_EXPERTS), 1).astype(F32)
    hots, sels, ws = [], [], []
    for _ in range(TOP_K):
        mx = jnp.max(biased, axis=-1, keepdims=True)
        sel = jnp.min(jnp.where(biased == mx, lane, float(N_EXPERTS)), axis=-1, keepdims=True)
        hot = lane == sel
        hots.append(hot)
        sels.append(sel)
        ws.append(jnp.sum(jnp.where(hot, scores, 0.0), axis=-1, keepdims=True))
        biased = jnp.where(hot, -jnp.inf, biased)
    wsum = ws[0]
    chosen = hots[0]
    for r in range(1, TOP_K):
        wsum = wsum + ws[r]
        chosen = chosen | hots[r]
    chosen_f = jnp.where(chosen, 1.0, 0.0)
    ti = lax.broadcasted_iota(jnp.int32, (tm, tm), 0)
    si = lax.broadcasted_iota(jnp.int32, (tm, tm), 1)
    before = jnp.where(si < ti, 1.0, 0.0).astype(BF16)
    rank = carry_scr[0:1, :] + _dot(before, chosen_f.astype(BF16))
    out_lane = lax.broadcasted_iota(jnp.int32, (tm, 128), 1)
    idx_o = jnp.zeros((tm, 128), jnp.int32)
    pos_o = jnp.zeros((tm, 128), jnp.int32)
    wn_o = jnp.zeros((tm, 128), F32)
    for r in range(TOP_K):
        pos_r = jnp.sum(jnp.where(hots[r], rank, 0.0), axis=-1, keepdims=True).astype(jnp.int32)
        idx_o = jnp.where(out_lane == r, sels[r].astype(jnp.int32), idx_o)
        pos_o = jnp.where(out_lane == r, pos_r, pos_o)
        wn_o = jnp.where(out_lane == r, ws[r] / wsum * ROUTE_SCALE, wn_o)
    idx_ref[...] = idx_o
    pos_ref[...] = pos_o
    wn_ref[...] = wn_o
    total = carry_scr[0:1, :] + jnp.sum(chosen_f, axis=0, keepdims=True)
    carry_scr[...] = jnp.broadcast_to(total, carry_scr.shape)
    cnt_ref[...] = jnp.broadcast_to(total, cnt_ref.shape).astype(jnp.int32)


def _router(h2, w_router, e_bias):
    n, d = h2.shape
    tm = TOK_TILE
    row = lambda i: (i, 0)
    const = lambda i: (0, 0)
    return pl.pallas_call(
        functools.partial(_router_kernel, tm=tm),
        grid=(n // tm,),
        in_specs=[pl.BlockSpec((tm, d), row), pl.BlockSpec((d, N_EXPERTS), const),
                  pl.BlockSpec((1, N_EXPERTS), const)],
        out_specs=[pl.BlockSpec((tm, 128), row), pl.BlockSpec((tm, 128), row), pl.BlockSpec((tm, 128), row),
                   pl.BlockSpec((8, N_EXPERTS), const)],
        out_shape=[jax.ShapeDtypeStruct((n, 128), jnp.int32), jax.ShapeDtypeStruct((n, 128), F32),
                   jax.ShapeDtypeStruct((n, 128), jnp.int32), jax.ShapeDtypeStruct((8, N_EXPERTS), jnp.int32)],
        scratch_shapes=[pltpu.VMEM((8, N_EXPERTS), F32)],
        compiler_params=_cparams(1),
        name="router",
    )(h2, w_router, e_bias)


def _slots_kernel(idx_ref, pos_ref, start_ref, dest_ref, *, tm):
    lane = lax.broadcasted_iota(jnp.int32, (tm, N_EXPERTS), 1)
    out_lane = lax.broadcasted_iota(jnp.int32, (tm, 128), 1)
    idx = idx_ref[...]
    pos = pos_ref[...]
    start = start_ref[0:1, :]
    dest = jnp.zeros((tm, 128), F32)
    for r in range(TOP_K):
        base = jnp.sum(jnp.where(lane == idx[:, r:r + 1], start, 0.0), axis=-1, keepdims=True)
        dest = jnp.where(out_lane == r, base + pos[:, r:r + 1].astype(F32), dest)
    dest_ref[0] = dest.T[0:TOP_K, :].astype(jnp.int32)


def _slots(idx, pos, start):
    n = idx.shape[0]
    tm = TOK_TILE
    return pl.pallas_call(
        functools.partial(_slots_kernel, tm=tm),
        grid=(n // tm,),
        in_specs=[pl.BlockSpec((tm, 128), lambda i: (i, 0)), pl.BlockSpec((tm, 128), lambda i: (i, 0)),
                  pl.BlockSpec((8, N_EXPERTS), lambda i: (0, 0))],
        out_specs=pl.BlockSpec((1, TOP_K, tm), lambda i: (i, 0, 0)),
        out_shape=jax.ShapeDtypeStruct((n // tm, TOP_K, tm), jnp.int32),
        compiler_params=_cparams(1),
        name="slots",
    )(idx, pos, start).reshape(-1)


def _dispatch_kernel(dest_ref, h_ref, xs_in_ref, xs_ref, pk_scr, sem, *, tm):
    del xs_in_ref
    pk_scr[...] = _pack_bf16(h_ref[...])

    def start(r, c):
        for k in range(TOP_K):
            pltpu.make_async_copy(pk_scr.at[pl.ds(r, 1), :], xs_ref.at[pl.ds(dest_ref[k * tm + r], 1), :], sem).start()
        return c

    lax.fori_loop(0, tm, start, 0)
    for k in range(TOP_K):
        pltpu.make_async_copy(pk_scr, xs_ref.at[pl.ds(0, tm), :], sem).wait()


def _dispatch(dest, h2, xs_init):
    n, d = h2.shape
    tm = TOK_TILE
    return pl.pallas_call(
        functools.partial(_dispatch_kernel, tm=tm),
        grid=(n // tm,),
        in_specs=[pl.BlockSpec((TOP_K * tm,), lambda i: (i,), memory_space=pltpu.SMEM),
                  pl.BlockSpec((tm, d), lambda i: (i, 0)),
                  pl.BlockSpec(memory_space=pl.ANY)],
        out_specs=pl.BlockSpec(memory_space=pl.ANY),
        out_shape=jax.ShapeDtypeStruct(xs_init.shape, xs_init.dtype),
        scratch_shapes=[pltpu.VMEM((tm, d // 2), jnp.uint32), pltpu.SemaphoreType.DMA(())],
        input_output_aliases={2: 0},
        compiler_params=_cparams(1),
        name="dispatch",
    )(dest, h2, xs_init)


def _expert_kernel(blk_e_ref, n_used_ref, xs_ref, wi_ref, wo_ref, y_ref, wi_scr, wo_scr):
    i = pl.program_id(0)
    used = i < n_used_ref[0]

    @pl.when(used & ((i == 0) | (blk_e_ref[i] != blk_e_ref[jnp.maximum(i - 1, 0)])))
    def _():
        wi_scr[...] = wi_ref[0, 0].astype(BF16)
        wo_scr[...] = wo_ref[0, 0].astype(BF16)

    @pl.when(used)
    def _():
        x = jnp.concatenate(_unpack_bf16(xs_ref[...]), axis=1).astype(BF16)
        ag = _dot(x, wi_scr[...])
        a = ag[:, :E_HIDDEN]
        g = ag[:, E_HIDDEN:]
        y_ref[...] = _pack_bf16(_dot((a * _sigmoid(a) * g).astype(BF16), wo_scr[...]))

    @pl.when(jnp.logical_not(used))
    def _():
        y_ref[...] = jnp.zeros_like(y_ref)


def _experts(blk_e, n_used, xs, w_e_in, w_e_out, layer):
    n_slots, dp = xs.shape
    d = 2 * dp
    n_blk = n_slots // MOE_ROWS

    def blk(i, be, nu):
        return (jnp.minimum(i, nu[0] - 1), 0)

    def wsel(i, be, nu):
        return (layer, be[jnp.minimum(i, nu[0] - 1)], 0, 0)

    return pl.pallas_call(
        _expert_kernel,
        grid_spec=pltpu.PrefetchScalarGridSpec(
            num_scalar_prefetch=2,
            grid=(n_blk,),
            in_specs=[pl.BlockSpec((MOE_ROWS, dp), blk),
                      pl.BlockSpec((1, 1, d, 2 * E_HIDDEN), wsel),
                      pl.BlockSpec((1, 1, E_HIDDEN, d), wsel)],
            out_specs=pl.BlockSpec((MOE_ROWS, dp), lambda i, be, nu: (i, 0)),
            scratch_shapes=[pltpu.VMEM((d, 2 * E_HIDDEN), BF16), pltpu.VMEM((E_HIDDEN, d), BF16)],
        ),
        out_shape=jax.ShapeDtypeStruct((n_slots, dp), jnp.uint32),
        compiler_params=_cparams(1),
        name="experts",
    )(blk_e, n_used, xs, w_e_in, w_e_out)


def _combine_kernel(dest_ref, y_ref, h_ref, x1_ref, wn_ref, ml_ref, mc_ref, wsi_ref, wso_ref, ln_ref,
                    x2_ref, g_scr, sem, *, tm, n_lat, n_tok, alpha):
    def start(r, c):
        for k in range(TOP_K):
            pltpu.make_async_copy(y_ref.at[pl.ds(dest_ref[k * tm + r], 1), :], g_scr.at[k, pl.ds(r, 1), :], sem).start()
        return c

    lax.fori_loop(0, tm, start, 0)
    h = h_ref[...]
    ag = _dot(h.astype(BF16), wsi_ref[...])
    a = ag[:, :S_HIDDEN]
    g = ag[:, S_HIDDEN:]
    f = _dot((a * _sigmoid(a) * g).astype(BF16), wso_ref[...])
    wn = wn_ref[...]
    f_hi = f[:, :D_MODEL // 2]
    f_lo = f[:, D_MODEL // 2:]
    for k in range(TOP_K):
        pltpu.make_async_copy(y_ref.at[pl.ds(0, tm), :], g_scr.at[k], sem).wait()
    for k in range(TOP_K):
        y_hi, y_lo = _unpack_bf16(g_scr[k])
        f_hi = f_hi + wn[:, k:k + 1] * y_hi
        f_lo = f_lo + wn[:, k:k + 1] * y_lo
    f = jnp.concatenate([f_hi, f_lo], axis=1)
    rows = (pl.program_id(0) * tm) % n_tok + lax.broadcasted_iota(jnp.int32, (tm, 1), 0)
    is_ctx = rows >= n_lat
    u = alpha * x1_ref[...] + _pick_mod(is_ctx, mc_ref, ml_ref, 5) * f
    x2_ref[...] = _layer_norm(u, ln_ref[0:1, :], ln_ref[1:2, :])


def _combine(dest, y, h2, x1, wn, mod_lat, mod_ctx, ws_in, ws_out, ln2, *, n_lat, n_tok, alpha):
    n, d = h2.shape
    tm = TOK_TILE
    tiles_per_batch = n_tok // tm
    row = lambda i: (i, 0)
    const = lambda i: (0, 0)
    return pl.pallas_call(
        functools.partial(_combine_kernel, tm=tm, n_lat=n_lat, n_tok=n_tok, alpha=alpha),
        grid=(n // tm,),
        in_specs=[pl.BlockSpec((TOP_K * tm,), lambda i: (i,), memory_space=pltpu.SMEM),
                  pl.BlockSpec(memory_space=pl.ANY),
                  pl.BlockSpec((tm, d), row),
                  pl.BlockSpec((tm, d), row),
                  pl.BlockSpec((tm, 128), row),
                  pl.BlockSpec((1, 8, d), lambda i: (i // tiles_per_batch, 0, 0)),
                  pl.BlockSpec((8, d), const),
                  pl.BlockSpec((d, 2 * S_HIDDEN), const),
                  pl.BlockSpec((S_HIDDEN, d), const),
                  pl.BlockSpec((8, d), const)],
        out_specs=pl.BlockSpec((tm, d), row),
        out_shape=jax.ShapeDtypeStruct((n, d), F32),
        scratch_shapes=[pltpu.VMEM((TOP_K, tm, d // 2), jnp.uint32), pltpu.SemaphoreType.DMA(())],
        compiler_params=_cparams(1),
        name="combine",
    )(dest, y, h2, x1, wn, mod_lat, mod_ctx, ws_in, ws_out, ln2)


def _rope_tables(n_lat, n_ctx):
    rows = n_lat // GRID_W
    row = jnp.repeat(jnp.arange(rows, dtype=F32), GRID_W)
    col = jnp.tile(jnp.arange(GRID_W, dtype=F32), rows)
    n_freq = A_DIM // 4
    inv = jnp.power(ROPE_THETA, -jnp.arange(n_freq, dtype=F32) / n_freq)
    ang_r = row[:, None] * inv
    ang_c = col[:, None] * inv
    cos = jnp.concatenate([jnp.cos(ang_r), jnp.cos(ang_r), jnp.cos(ang_c), jnp.cos(ang_c)], axis=1)
    sin = jnp.concatenate([-jnp.sin(ang_r), jnp.sin(ang_r), -jnp.sin(ang_c), jnp.sin(ang_c)], axis=1)
    cos = jnp.concatenate([jnp.tile(cos, (1, 2)), jnp.ones((n_ctx, 128), F32)], axis=0)
    sin = jnp.concatenate([jnp.tile(sin, (1, 2)), jnp.zeros((n_ctx, 128), F32)], axis=0)
    return cos, sin


def _row_tile(n_tok):
    for tm in (1280, 640, 256):
        if n_tok % tm == 0:
            return tm
    raise ValueError(f"unsupported token count {n_tok}")


def _pad_rows(a, rows):
    return jnp.concatenate([a, jnp.zeros((rows - a.shape[0],) + a.shape[1:], a.dtype)], axis=0)


def kernel(x, c, ctx, c_ctx, ada_w, ada_b, w_in, b_gates, conv_qk, lam, subln_a, norm_m, w_br_a, w_br_m, w_out,
           ln1_g, ln1_b, ln2_g, ln2_b, w_router, e_bias, w_e_in, w_e_out, ws_in, ws_out):
    bsz, n_lat, d = x.shape
    n_ctx = ctx.shape[1]
    n_tok = n_lat + n_ctx
    depth = ada_w.shape[0]
    assert d == D_MODEL and n_ctx == MLSTM_CHUNK and n_lat % 512 == 0 and bsz + 1 <= 8
    alpha = (2 * depth) ** 0.25
    tm = _row_tile(n_tok)
    tq = 512
    ck = next(c for c in (1280, 256) if n_tok % c == 0)
    n_all = bsz * n_tok
    n_assign = n_all * TOP_K
    n_blk = -(-n_assign // MOE_ROWS) + N_EXPERTS
    n_slots = n_blk * MOE_ROWS

    cond = _pad_rows(jnp.concatenate([c, c_ctx[None, :]], axis=0), 8)
    mods = _ada(cond, ada_w, ada_b).reshape(depth, 8, N_MOD, d)
    cos_t, sin_t = _rope_tables(n_lat, n_ctx)
    xc = jnp.concatenate([x, ctx], axis=1)
    g_lo = 6 * COL_TILE
    g_hi = g_lo + 4 * M_HEADS

    for l in range(depth):
        lam_init = 0.8 - 0.6 * math.exp(-0.3 * l)
        mod_lat = _pad_rows(mods[l, :bsz].transpose(1, 0, 2), 8).transpose(1, 0, 2)
        mod_ctx = _pad_rows(mods[l, bsz], 8)
        w_main = jnp.concatenate([w_in[l, :, :g_lo], w_in[l, :, g_hi:]], axis=1).astype(BF16)
        w_gates = jnp.pad(w_in[l, :, g_lo:g_hi], ((0, 0), (0, 128 - 4 * M_HEADS)))
        bg = jnp.pad(b_gates[l], (0, 128 - 4 * M_HEADS))[None, :]
        conv_w = _pad_rows(conv_qk[l], 8)
        lam_p = jnp.pad(lam[l], ((0, 4), (0, 128 - A_DIM)))
        subln = subln_a[l][None, :]

        z, gates = _proj(xc, mod_lat, mod_ctx, w_main, w_gates, bg, cos_t, sin_t, conv_w, n_lat=n_lat, tm=tm)
        oa_lat = _attention(z, lam_p, subln, n_lat=n_lat, tq=tq, ck=ck, lat_queries=True, lam_init=lam_init)
        oa_ctx = _attention(z, lam_p, subln, n_lat=n_lat, tq=n_ctx, ck=ck, lat_queries=False, lam_init=lam_init)
        gates_t = gates[:, :, :4 * M_HEADS].transpose(0, 2, 1)
        hf, hb = _mlstm(z, gates, gates_t)
        ln1 = _pad_rows(jnp.stack([ln1_g[l], ln1_b[l]]), 8)
        x1, h2 = _merge(xc, oa_lat, oa_ctx, hf, hb, z, mod_lat, mod_ctx, norm_m[l][None, :], w_br_a[l].astype(BF16),
                        w_br_m[l].astype(BF16), w_out[l].astype(BF16), ln1, n_lat=n_lat, tm=TOK_TILE, alpha=alpha)

        h2f = h2.reshape(n_all, d)
        idx, wn, pos, cnt = _router(h2f, w_router[l], e_bias[l][None, :])
        counts = cnt[0]
        padded = (counts + MOE_ROWS - 1) // MOE_ROWS * MOE_ROWS
        pend = jnp.cumsum(padded)
        pstart = pend - padded
        dest = _slots(idx, pos, jnp.broadcast_to(pstart.astype(F32)[None, :], (8, N_EXPERTS)))
        blk_row = jnp.arange(n_blk, dtype=jnp.int32) * MOE_ROWS
        blk_e = jnp.minimum(jnp.sum(pend[None, :] <= blk_row[:, None], axis=1), N_EXPERTS - 1).astype(jnp.int32)
        n_used = (pend[-1:] // MOE_ROWS).astype(jnp.int32)
        xs = _dispatch(dest, h2f, jnp.zeros((n_slots, d // 2), jnp.uint32))
        y = _experts(blk_e, n_used, xs, w_e_in, w_e_out, l)
        ln2 = _pad_rows(jnp.stack([ln2_g[l], ln2_b[l]]), 8)
        x2 = _combine(dest, y, h2f, x1.reshape(n_all, d), wn, mod_lat, mod_ctx, ws_in[l].astype(BF16),
                      ws_out[l].astype(BF16), ln2, n_lat=n_lat, n_tok=n_tok, alpha=alpha)
        xc = x2.reshape(bsz, n_tok, d)
    return xc[:, :n_lat]
```

```python
import functools
import math

import jax
import jax.numpy as jnp
from jax import lax
from jax.experimental import pallas as pl
from jax.experimental.pallas import tpu as pltpu

F32 = jnp.float32
BF16 = jnp.bfloat16

D_MODEL = 1024
GRID_W = 64
A_HEADS = 8
A_DIM = 64
A_VDIM = 2 * A_DIM
ROPE_THETA = 10000.0
Q_SCALE = A_DIM ** -0.5 * math.log2(math.e)
M_HEADS = 4
M_QK = 128
M_V = 256
M_VAUG = M_V + 128
N_EXPERTS = 256
TOP_K = 8
E_HIDDEN = 256
S_HIDDEN = 256
ROUTE_SCALE = 2.5
LN_EPS = 1e-5
RMS_EPS = 1e-6
N_MOD = 6
COL_TILE = 1024
N_COL_TILES = 8
PROJ_GROUP = 512
ATTN_UNROLL = 12
ATTN_TILES = 1
MLSTM_CHUNK = 256
MOE_ROWS = 512
TOK_TILE = 256
VMEM_LIMIT = 56 * 1024 * 1024


def _cparams(n_axes):
    return pltpu.CompilerParams(dimension_semantics=("arbitrary",) * n_axes,
                                vmem_limit_bytes=VMEM_LIMIT)


def _dot(a, b):
    return jnp.dot(a, b, preferred_element_type=F32)


def _dot_nt(a, b):
    return lax.dot_general(a, b, (((1,), (1,)), ((), ())), preferred_element_type=F32)


def _split_bf16(a):
    hi = a.astype(BF16)
    lo = (a - hi.astype(F32)).astype(BF16)
    return hi, lo


def _dot3(a, b):
    ah, al = _split_bf16(a)
    bh, bl = _split_bf16(b)
    return _dot(ah, bh) + _dot(al, bh) + _dot(ah, bl)


def _pack_bf16(x):
    n = x.shape[1] // 2
    hi = lax.bitcast_convert_type(x[:, :n].astype(BF16).astype(F32), jnp.uint32)
    lo = lax.bitcast_convert_type(x[:, n:].astype(BF16).astype(F32), jnp.uint32)
    return hi | (lo >> 16)


def _unpack_bf16(p):
    hi = lax.bitcast_convert_type(p & jnp.uint32(0xFFFF0000), F32)
    lo = lax.bitcast_convert_type(p << 16, F32)
    return hi, lo


def _sigmoid(v):
    return 1.0 / (1.0 + jnp.exp(-v))


def _log_sigmoid(v):
    return jnp.minimum(v, 0.0) - jnp.log(1.0 + jnp.exp(-jnp.abs(v)))


def _layer_norm(u, g, b):
    mu = jnp.mean(u, axis=-1, keepdims=True)
    var = jnp.mean(jnp.square(u - mu), axis=-1, keepdims=True)
    return (u - mu) * lax.rsqrt(var + LN_EPS) * g + b


def _pick_mod(is_ctx, mc_ref, ml_ref, k):
    return jnp.where(is_ctx, mc_ref[k:k + 1, :], ml_ref[0, k:k + 1, :])


def _ada_kernel(cond_ref, w_ref, b_ref, o_ref):
    c = cond_ref[...]
    c = c * _sigmoid(c)
    o_ref[0] = _dot3(c, w_ref[0]) + b_ref[0]


def _ada(cond, ada_w, ada_b):
    depth, d, width = ada_w.shape
    tn = 512
    return pl.pallas_call(
        _ada_kernel,
        grid=(depth, width // tn),
        in_specs=[pl.BlockSpec((8, d), lambda l, j: (0, 0)),
                  pl.BlockSpec((1, d, tn), lambda l, j: (l, 0, j)),
                  pl.BlockSpec((1, 1, tn), lambda l, j: (l, 0, j))],
        out_specs=pl.BlockSpec((1, 8, tn), lambda l, j: (l, 0, j)),
        out_shape=jax.ShapeDtypeStruct((depth, 8, width), F32),
        compiler_params=_cparams(2),
        name="ada",
    )(cond, ada_w, ada_b.reshape(depth, 1, width))


def _proj_kernel(x_ref, xp_ref, xn_ref, ml_ref, mc_ref, w_ref, wg_ref, bg_ref, cos_ref, sin_ref,
                 conv_ref, z_ref, gates_ref, h_scr, halo_scr, *, tm, n_lat, n_tok):
    i = pl.program_id(1)
    j = pl.program_id(2)
    row0 = i * tm

    def modulate(xv, rows):
        is_ctx = rows >= n_lat
        return xv * (1.0 + _pick_mod(is_ctx, mc_ref, ml_ref, 1)) + _pick_mod(is_ctx, mc_ref, ml_ref, 0)

    @pl.when(j == 0)
    def _():
        rows = row0 + lax.broadcasted_iota(jnp.int32, (tm, 1), 0)
        h = modulate(x_ref[0], rows)
        h_scr[...] = h.astype(BF16)
        r8 = lax.broadcasted_iota(jnp.int32, (8, 1), 0)
        halo_scr[0:8, :] = modulate(xp_ref[0], row0 - 8 + r8)
        halo_scr[8:16, :] = modulate(xn_ref[0], row0 + tm + r8)
        gates_ref[0] = _dot3(h, wg_ref[...]) + bg_ref[...]

    def col_groups(fn):
        for lo in range(0, COL_TILE, PROJ_GROUP):
            fn(_dot(h_scr[...], w_ref[:, lo:lo + PROJ_GROUP]), lo)

    @pl.when(j <= 1)
    def _():
        lane = lax.broadcasted_iota(jnp.int32, (tm, 128), 1)
        first = (lane % 32) < 16
        scale = jnp.where(j == 0, Q_SCALE, 1.0).astype(F32)
        cos = cos_ref[...] * scale
        sin = sin_ref[...] * scale

        def rope(zt, lo):
            for g in range(PROJ_GROUP // 128):
                zg = zt[:, g * 128:(g + 1) * 128]
                partner = jnp.where(first, pltpu.roll(zg, 128 - 16, 1), pltpu.roll(zg, 16, 1))
                z_ref[0, :, lo + g * 128:lo + (g + 1) * 128] = (zg * cos + partner * sin).astype(BF16)

        col_groups(rope)

    @pl.when(j == 3)
    def _():
        loc = lax.broadcasted_iota(jnp.int32, (tm, 1), 0)
        rows = row0 + loc
        seg_start = (rows == 0) | (rows == n_lat)
        seg_end = (rows == n_lat - 1) | (rows == n_tok - 1)
        halo = halo_scr[...].astype(BF16)

        def conv(zt, lo):
            zh = _dot(halo, w_ref[:, lo:lo + PROJ_GROUP])
            zd = jnp.where(loc == 0, zh[7:8, :], pltpu.roll(zt, 1, 0))
            zd = jnp.where(seg_start, 0.0, zd)
            zu = jnp.where(loc == tm - 1, zh[8:9, :], pltpu.roll(zt, tm - 1, 0))
            zu = jnp.where(seg_end, 0.0, zu)
            cw = conv_ref[:, lo:lo + PROJ_GROUP]
            y = zd * cw[0:1, :] + zt * cw[1:2, :] + zu * cw[2:3, :]
            y = y * _sigmoid(y)
            if lo >= M_HEADS * M_QK:
                y = y * (M_QK ** -0.5)
            z_ref[0, :, lo:lo + PROJ_GROUP] = y.astype(BF16)

        col_groups(conv)

    @pl.when((j == 2) | (j >= 4))
    def _():
        def plain(zt, lo):
            z_ref[0, :, lo:lo + PROJ_GROUP] = zt.astype(BF16)

        col_groups(plain)


def _proj(xc, mod_lat, mod_ctx, w_main, w_gates, b_gates, cos_t, sin_t, conv_w, *, n_lat, tm):
    bsz, n_tok, d = xc.shape
    nt8 = n_tok // 8
    kern = functools.partial(_proj_kernel, tm=tm, n_lat=n_lat, n_tok=n_tok)
    return pl.pallas_call(
        kern,
        grid=(bsz, n_tok // tm, N_COL_TILES),
        in_specs=[
            pl.BlockSpec((1, tm, d), lambda b, i, j: (b, i, 0)),
            pl.BlockSpec((1, 8, d), lambda b, i, j: (b, jnp.maximum(i * (tm // 8) - 1, 0), 0)),
            pl.BlockSpec((1, 8, d), lambda b, i, j: (b, jnp.minimum((i + 1) * (tm // 8), nt8 - 1), 0)),
            pl.BlockSpec((1, 8, d), lambda b, i, j: (b, 0, 0)),
            pl.BlockSpec((8, d), lambda b, i, j: (0, 0)),
            pl.BlockSpec((d, COL_TILE), lambda b, i, j: (0, j)),
            pl.BlockSpec((d, 128), lambda b, i, j: (0, 0)),
            pl.BlockSpec((1, 128), lambda b, i, j: (0, 0)),
            pl.BlockSpec((tm, 128), lambda b, i, j: (i, 0)),
            pl.BlockSpec((tm, 128), lambda b, i, j: (i, 0)),
            pl.BlockSpec((8, COL_TILE), lambda b, i, j: (0, 0)),
        ],
        out_specs=[
            pl.BlockSpec((1, tm, COL_TILE), lambda b, i, j: (b, i, j)),
            pl.BlockSpec((1, tm, 128), lambda b, i, j: (b, i, 0)),
        ],
        out_shape=[jax.ShapeDtypeStruct((bsz, n_tok, N_COL_TILES * COL_TILE), BF16),
                   jax.ShapeDtypeStruct((bsz, n_tok, 128), F32)],
        scratch_shapes=[pltpu.VMEM((tm, d), BF16), pltpu.VMEM((16, d), F32)],
        compiler_params=_cparams(3),
        name="proj",
    )(xc, xc, xc, mod_lat, mod_ctx, w_main, w_gates, b_gates, cos_t, sin_t, conv_w)


def _attn_kernel(lam_ref, g_ref, q_ref, k_ref, v_ref, o_ref, *s_scr, tq, tiles, ck, n_chunks, kv_lo, lam_init):
    lane = lax.broadcasted_iota(jnp.int32, (1, A_VDIM), 1)
    slots = (s_scr[0:2], s_scr[2:4])

    def masked_q(t):
        q = q_ref[0, t * tq:(t + 1) * tq, :]
        zero = jnp.zeros_like(q)
        return jnp.where(lane < A_DIM, q, zero), jnp.where(lane >= A_DIM, q, zero)

    ones_blk = jnp.where(lax.broadcasted_iota(jnp.int32, (ck, A_VDIM), 1) == 0, 1.0, 0.0).astype(BF16)

    def kv_start(c):
        return pl.multiple_of(kv_lo + c * ck, 128)

    def scores(qs, c, slot):
        k = k_ref[0, pl.ds(kv_start(c), ck), :]
        for qj, s_ref in zip(qs, slots[slot]):
            s_ref[...] = _dot_nt(qj, k)

    def absorb(c, slot, carry):
        v_aug = jnp.concatenate([v_ref[0, pl.ds(kv_start(c), ck), :], ones_blk], axis=1)
        out = []
        for s_ref, (m, acc) in zip(slots[slot], carry):
            s = s_ref[...]
            m_new = jnp.maximum(m, jnp.max(s, axis=-1, keepdims=True))
            alpha = jnp.exp2(m - m_new)
            p = jnp.exp2(s - m_new)
            acc = alpha * acc + _dot(p.astype(BF16), v_aug)
            out.append((m_new, acc))
        return tuple(out)

    lam = lam_ref[...]
    s1 = jnp.sum(lam[0:1, :] * lam[1:2, :], axis=-1, keepdims=True)
    s2 = jnp.sum(lam[2:3, :] * lam[3:4, :], axis=-1, keepdims=True)
    lam_val = jnp.exp(s1) - jnp.exp(s2) + lam_init

    def finalize(t, carry):
        (_, acc0), (_, acc1) = carry
        o = (acc0[:, :A_VDIM] / acc0[:, A_VDIM:A_VDIM + 1]
             - lam_val * (acc1[:, :A_VDIM] / acc1[:, A_VDIM:A_VDIM + 1]))
        y = o * lax.rsqrt(jnp.mean(o * o, axis=-1, keepdims=True) + RMS_EPS) * g_ref[...] * (1.0 - lam_init)
        o_ref[0, t * tq:(t + 1) * tq, :] = y.astype(BF16)

    one = (jnp.full((tq, 1), -jnp.inf, F32), jnp.zeros((tq, 2 * A_VDIM), F32))
    qs = masked_q(0)
    scores(qs, 0, 0)
    for t in range(tiles):
        base = t * n_chunks

        def run(c0, count, carry, qs=qs, base=base):
            for u in range(count):
                scores(qs, c0 + u + 1, (base + u + 1) % 2)
                carry = absorb(c0 + u, (base + u) % 2, carry)
            return carry

        trips = (n_chunks - 1) // ATTN_UNROLL
        carry = (one, one)
        if trips > 0:
            carry = lax.fori_loop(0, trips, lambda i, cr, run=run: run(i * ATTN_UNROLL, ATTN_UNROLL, cr), carry)
        rest = n_chunks - 1 - trips * ATTN_UNROLL
        carry = run(trips * ATTN_UNROLL, rest, carry)
        if t + 1 < tiles:
            qs = masked_q(t + 1)
            scores(qs, 0, (base + n_chunks) % 2)
        carry = absorb(n_chunks - 1, (base + n_chunks - 1) % 2, carry)
        finalize(t, carry)


def _attention(z, lam_p, subln, *, n_lat, tq, ck, lat_queries, lam_init):
    bsz, n_tok, _ = z.shape
    n_ctx = n_tok - n_lat
    n_q = n_lat if lat_queries else n_ctx
    if not lat_queries:
        ck = n_ctx
    tiles = ATTN_TILES if n_q % (tq * ATTN_TILES) == 0 else 1
    rows = tq * tiles
    qb = 0 if lat_queries else n_lat // rows
    kern = functools.partial(_attn_kernel, tq=tq, tiles=tiles, ck=ck, n_chunks=(n_tok if lat_queries else n_ctx) // ck,
                             kv_lo=0 if lat_queries else n_lat, lam_init=lam_init)
    return pl.pallas_call(
        kern,
        grid=(bsz, A_HEADS, n_q // rows),
        in_specs=[
            pl.BlockSpec((8, 128), lambda b, h, i: (0, 0)),
            pl.BlockSpec((1, 128), lambda b, h, i: (0, 0)),
            pl.BlockSpec((1, rows, A_VDIM), lambda b, h, i: (b, qb + i, h)),
            pl.BlockSpec((1, n_tok, A_VDIM), lambda b, h, i: (b, 0, A_HEADS + h)),
            pl.BlockSpec((1, n_tok, A_VDIM), lambda b, h, i: (b, 0, 2 * A_HEADS + h)),
        ],
        out_specs=pl.BlockSpec((1, rows, A_VDIM), lambda b, h, i: (b, i, h)),
        out_shape=jax.ShapeDtypeStruct((bsz, n_q, A_HEADS * A_VDIM), BF16),
        scratch_shapes=[pltpu.VMEM((tq, ck), F32)] * 4,
        compiler_params=_cparams(3),
        name="attn_lat" if lat_queries else "attn_ctx",
    )(lam_p, subln, z, z, z)


def _mlstm_kernel(qf_ref, kf_ref, vf_ref, gf_ref, gtf_ref, qb_ref, kb_ref, vb_ref, gb_ref, gtb_ref,
                  hf_ref, hb_ref, c_scr, m_scr, *, t):
    step = pl.program_id(1)

    @pl.when(step == 0)
    def _():
        c_scr[...] = jnp.zeros_like(c_scr)
        m_scr[...] = jnp.zeros_like(m_scr)

    ti = lax.broadcasted_iota(jnp.int32, (t, t), 0)
    si = lax.broadcasted_iota(jnp.int32, (t, t), 1)
    lane = lax.broadcasted_iota(jnp.int32, (t, 128), 1)
    ones_blk = jnp.where(lane == 0, 1.0, 0.0).astype(BF16)
    dirs = ((qf_ref, kf_ref, vf_ref, gf_ref, gtf_ref, hf_ref), (qb_ref, kb_ref, vb_ref, gb_ref, gtb_ref, hb_ref))
    units = []
    for d, (q_ref, k_ref, v_ref, g_ref, gt_ref, h_ref) in enumerate(dirs):
        seen = (si <= ti) if d == 0 else (si >= ti)
        seen_t = (ti <= si) if d == 0 else (ti >= si)
        lf_cols = _log_sigmoid(g_ref[0])
        lf_rows = _log_sigmoid(gt_ref[0])
        for hd in range(M_HEADS):
            u = dict(idx=d * M_HEADS + hd, h_ref=h_ref, hd=hd)
            u["q"] = q_ref[0, :, hd * M_QK:(hd + 1) * M_QK]
            u["k"] = k_ref[0, :, hd * M_QK:(hd + 1) * M_QK]
            v = v_ref[0, :, hd * M_V:(hd + 1) * M_V]
            u["v_aug"] = jnp.concatenate([v, ones_blk], axis=1)
            li = d * 2 * M_HEADS + hd
            lf = li + M_HEADS
            i_col = g_ref[0, :, li:li + 1]
            i_row = gt_ref[0, li:li + 1, :]
            lf_col = lf_cols[:, lf:lf + 1]
            lf_row = lf_rows[lf:lf + 1, :]
            b_col = jnp.sum(jnp.where(seen, lf_row, 0.0), axis=1, keepdims=True)
            b_row = jnp.sum(jnp.where(seen_t, lf_col, 0.0), axis=0, keepdims=True)
            b_end = jnp.sum(lf_row, axis=1, keepdims=True)
            m_old = m_scr[u["idx"], 0:1, 0:1]
            dmat = jnp.where(seen, b_col - b_row + i_row, -jnp.inf)
            u["m_t"] = jnp.maximum(b_col + m_old, jnp.max(dmat, axis=1, keepdims=True))
            u["dexp"] = jnp.exp(dmat - u["m_t"])
            u["iscale"] = jnp.exp(b_col + m_old - u["m_t"])
            g_col = b_end - b_col + i_col
            u["m_new"] = jnp.maximum(b_end + m_old, jnp.max(g_col, axis=0, keepdims=True))
            u["w_col"] = jnp.exp(g_col - u["m_new"])
            u["decay"] = jnp.exp(b_end + m_old - u["m_new"])
            units.append(u)
    for u in units:
        u["qk"] = _dot_nt(u["q"], u["k"])
        u["c_aug"] = c_scr[u["idx"]]
        u["inter"] = _dot(u["q"], u["c_aug"].astype(BF16))
    for u in units:
        wv = (u["w_col"] * u["v_aug"].astype(F32)).astype(BF16)
        k_t = u["k"].astype(F32).T.astype(BF16)
        c_scr[u["idx"]] = u["decay"] * u["c_aug"] + _dot(k_t, wv)
        m_scr[u["idx"]] = jnp.broadcast_to(u["m_new"], (8, 128))
    for u in units:
        intra = _dot((u["qk"] * u["dexp"]).astype(BF16), u["v_aug"])
        num = u["iscale"] * u["inter"][:, :M_V] + intra[:, :M_V]
        den = u["iscale"] * u["inter"][:, M_V:M_V + 1] + intra[:, M_V:M_V + 1]
        hd = u["hd"]
        u["h_ref"][0, :, hd * M_V:(hd + 1) * M_V] = num / jnp.maximum(jnp.abs(den), jnp.exp(-u["m_t"]))


def _mlstm(z, gates, gates_t):
    bsz, n_tok, _ = z.shape
    t = MLSTM_CHUNK
    nc = n_tok // t
    qcol = 3 * COL_TILE // (M_HEADS * M_QK)

    def cf(s):
        return jnp.where(s == 0, nc - 1, s - 1)

    def cb(s):
        return nc - 1 - s

    def specs(c):
        return [
            pl.BlockSpec((1, t, M_HEADS * M_QK), lambda b, s: (b, c(s), qcol)),
            pl.BlockSpec((1, t, M_HEADS * M_QK), lambda b, s: (b, c(s), qcol + 1)),
            pl.BlockSpec((1, t, M_HEADS * M_V), lambda b, s: (b, c(s), 4)),
            pl.BlockSpec((1, t, 128), lambda b, s: (b, c(s), 0)),
            pl.BlockSpec((1, 16, t), lambda b, s: (b, 0, c(s))),
        ]

    out_sd = jax.ShapeDtypeStruct((bsz, n_tok, M_HEADS * M_V), F32)
    return pl.pallas_call(
        functools.partial(_mlstm_kernel, t=t),
        grid=(bsz, nc),
        in_specs=specs(cf) + specs(cb),
        out_specs=[pl.BlockSpec((1, t, M_HEADS * M_V), lambda b, s: (b, cf(s), 0)),
                   pl.BlockSpec((1, t, M_HEADS * M_V), lambda b, s: (b, cb(s), 0))],
        out_shape=[out_sd, out_sd],
        scratch_shapes=[pltpu.VMEM((2 * M_HEADS, M_QK, M_VAUG), F32), pltpu.VMEM((2 * M_HEADS, 8, 128), F32)],
        compiler_params=_cparams(2),
        name="mlstm",
    )(z, z, z, gates, gates_t, z, z, z, gates, gates_t)


def _merge_kernel(x_ref, oal_ref, oac_ref, hf_ref, hb_ref, om_ref, ga_ref, gb_ref, ml_ref, mc_ref, nm_ref,
                  wa_ref, wm_ref, wo_ref, ln_ref, x1_ref, h2_ref, *, tm, n_lat, alpha):
    i = pl.program_id(1)
    rows = i * tm + lax.broadcasted_iota(jnp.int32, (tm, 1), 0)
    is_ctx = rows >= n_lat
    oa = jnp.where(i * tm >= n_lat, oac_ref[0], oal_ref[0])
    hm = hf_ref[0] + hb_ref[0]
    parts = []
    for hd in range(M_HEADS):
        seg = hm[:, hd * M_V:(hd + 1) * M_V]
        parts.append(seg * lax.rsqrt(jnp.mean(seg * seg, axis=-1, keepdims=True) + RMS_EPS))
    hm = jnp.concatenate(parts, axis=1) * nm_ref[...] * _sigmoid(om_ref[0].astype(F32))
    ya = _dot(oa, wa_ref[...])
    ym = _dot(hm.astype(BF16), wm_ref[...])
    y = _sigmoid(ga_ref[0].astype(F32)) * ya + _sigmoid(gb_ref[0].astype(F32)) * ym
    y = _dot(y.astype(BF16), wo_ref[...])
    u = alpha * x_ref[0] + _pick_mod(is_ctx, mc_ref, ml_ref, 2) * y
    x1 = _layer_norm(u, ln_ref[0:1, :], ln_ref[1:2, :])
    x1_ref[0] = x1
    h2_ref[0] = x1 * (1.0 + _pick_mod(is_ctx, mc_ref, ml_ref, 4)) + _pick_mod(is_ctx, mc_ref, ml_ref, 3)


def _merge(xc, oa_lat, oa_ctx, hf, hb, z, mod_lat, mod_ctx, norm_m, w_a, w_m, w_o, ln1, *, n_lat, tm, alpha):
    bsz, n_tok, d = xc.shape
    row = lambda b, i: (b, i, 0)
    const = lambda b, i: (0, 0)
    out_sd = jax.ShapeDtypeStruct((bsz, n_tok, d), F32)
    lat_tiles = n_lat // tm
    return pl.pallas_call(
        functools.partial(_merge_kernel, tm=tm, n_lat=n_lat, alpha=alpha),
        grid=(bsz, n_tok // tm),
        in_specs=[
            pl.BlockSpec((1, tm, d), row),
            pl.BlockSpec((1, tm, d), lambda b, i: (b, jnp.minimum(i, lat_tiles - 1), 0)),
            pl.BlockSpec((1, tm, d), lambda b, i: (b, jnp.maximum(i - lat_tiles, 0), 0)),
            pl.BlockSpec((1, tm, d), row),
            pl.BlockSpec((1, tm, d), row),
            pl.BlockSpec((1, tm, COL_TILE), lambda b, i: (b, i, 5)),
            pl.BlockSpec((1, tm, COL_TILE), lambda b, i: (b, i, 6)),
            pl.BlockSpec((1, tm, COL_TILE), lambda b, i: (b, i, 7)),
            pl.BlockSpec((1, 8, d), lambda b, i: (b, 0, 0)),
            pl.BlockSpec((8, d), const),
            pl.BlockSpec((1, d), const),
            pl.BlockSpec((d, d), const),
            pl.BlockSpec((d, d), const),
            pl.BlockSpec((d, d), const),
            pl.BlockSpec((8, d), const),
        ],
        out_specs=[pl.BlockSpec((1, tm, d), row), pl.BlockSpec((1, tm, d), row)],
        out_shape=[out_sd, out_sd],
        compiler_params=_cparams(2),
        name="merge",
    )(xc, oa_lat, oa_ctx, hf, hb, z, z, z, mod_lat, mod_ctx, norm_m, w_a, w_m, w_o, ln1)


def _router_kernel(h_ref, wr_ref, eb_ref, idx_ref, wn_ref, pos_ref, cnt_ref, carry_scr, *, tm):
    @pl.when(pl.program_id(0) == 0)
    def _():
        carry_scr[...] = jnp.zeros_like(carry_scr)

    scores = _sigmoid(_dot3(h_ref[...], wr_ref[...]))
    biased = scores + eb_ref[...]
    lane = lax.broadcasted_iota(jnp.int32, (tm, N_EXPERTS), 1).astype(F32)
    hots, sels, ws = [], [], []
    for _ in range(TOP_K):
        mx = jnp.max(biased, axis=-1, keepdims=True)
        sel = jnp.min(jnp.where(biased == mx, lane, float(N_EXPERTS)), axis=-1, keepdims=True)
        hot = lane == sel
        hots.append(hot)
        sels.append(sel)
        ws.append(jnp.sum(jnp.where(hot, scores, 0.0), axis=-1, keepdims=True))
        biased = jnp.where(hot, -jnp.inf, biased)
    wsum = ws[0]
    chosen = hots[0]
    for r in range(1, TOP_K):
        wsum = wsum + ws[r]
        chosen = chosen | hots[r]
    chosen_f = jnp.where(chosen, 1.0, 0.0)
    ti = lax.broadcasted_iota(jnp.int32, (tm, tm), 0)
    si = lax.broadcasted_iota(jnp.int32, (tm, tm), 1)
    before = jnp.where(si < ti, 1.0, 0.0).astype(BF16)
    rank = carry_scr[0:1, :] + _dot(before, chosen_f.astype(BF16))
    out_lane = lax.broadcasted_iota(jnp.int32, (tm, 128), 1)
    idx_o = jnp.zeros((tm, 128), jnp.int32)
    pos_o = jnp.zeros((tm, 128), jnp.int32)
    wn_o = jnp.zeros((tm, 128), F32)
    for r in range(TOP_K):
        pos_r = jnp.sum(jnp.where(hots[r], rank, 0.0), axis=-1, keepdims=True).astype(jnp.int32)
        idx_o = jnp.where(out_lane == r, sels[r].astype(jnp.int32), idx_o)
        pos_o = jnp.where(out_lane == r, pos_r, pos_o)
        wn_o = jnp.where(out_lane == r, ws[r] / wsum * ROUTE_SCALE, wn_o)
    idx_ref[...] = idx_o
    pos_ref[...] = pos_o
    wn_ref[...] = wn_o
    total = carry_scr[0:1, :] + jnp.sum(chosen_f, axis=0, keepdims=True)
    carry_scr[...] = jnp.broadcast_to(total, carry_scr.shape)
    cnt_ref[...] = jnp.broadcast_to(total, cnt_ref.shape).astype(jnp.int32)


def _router(h2, w_router, e_bias):
    n, d = h2.shape
    tm = TOK_TILE
    row = lambda i: (i, 0)
    const = lambda i: (0, 0)
    return pl.pallas_call(
        functools.partial(_router_kernel, tm=tm),
        grid=(n // tm,),
        in_specs=[pl.BlockSpec((tm, d), row), pl.BlockSpec((d, N_EXPERTS), const),
                  pl.BlockSpec((1, N_EXPERTS), const)],
        out_specs=[pl.BlockSpec((tm, 128), row), pl.BlockSpec((tm, 128), row), pl.BlockSpec((tm, 128), row),
                   pl.BlockSpec((8, N_EXPERTS), const)],
        out_shape=[jax.ShapeDtypeStruct((n, 128), jnp.int32), jax.ShapeDtypeStruct((n, 128), F32),
                   jax.ShapeDtypeStruct((n, 128), jnp.int32), jax.ShapeDtypeStruct((8, N_EXPERTS), jnp.int32)],
        scratch_shapes=[pltpu.VMEM((8, N_EXPERTS), F32)],
        compiler_params=_cparams(1),
        name="router",
    )(h2, w_router, e_bias)


def _slots_kernel(idx_ref, pos_ref, start_ref, dest_ref, *, tm):
    lane = lax.broadcasted_iota(jnp.int32, (tm, N_EXPERTS), 1)
    out_lane = lax.broadcasted_iota(jnp.int32, (tm, 128), 1)
    idx = idx_ref[...]
    pos = pos_ref[...]
    start = start_ref[0:1, :]
    dest = jnp.zeros((tm, 128), F32)
    for r in range(TOP_K):
        base = jnp.sum(jnp.where(lane == idx[:, r:r + 1], start, 0.0), axis=-1, keepdims=True)
        dest = jnp.where(out_lane == r, base + pos[:, r:r + 1].astype(F32), dest)
    dest_ref[0] = dest.T[0:TOP_K, :].astype(jnp.int32)


def _slots(idx, pos, start):
    n = idx.shape[0]
    tm = TOK_TILE
    return pl.pallas_call(
        functools.partial(_slots_kernel, tm=tm),
        grid=(n // tm,),
        in_specs=[pl.BlockSpec((tm, 128), lambda i: (i, 0)), pl.BlockSpec((tm, 128), lambda i: (i, 0)),
                  pl.BlockSpec((8, N_EXPERTS), lambda i: (0, 0))],
        out_specs=pl.BlockSpec((1, TOP_K, tm), lambda i: (i, 0, 0)),
        out_shape=jax.ShapeDtypeStruct((n // tm, TOP_K, tm), jnp.int32),
        compiler_params=_cparams(1),
        name="slots",
    )(idx, pos, start).reshape(-1)


def _dispatch_kernel(dest_ref, h_ref, xs_in_ref, xs_ref, pk_scr, sem, *, tm):
    del xs_in_ref
    pk_scr[...] = _pack_bf16(h_ref[...])

    def start(r, c):
        for k in range(TOP_K):
            pltpu.make_async_copy(pk_scr.at[pl.ds(r, 1), :], xs_ref.at[pl.ds(dest_ref[k * tm + r], 1), :], sem).start()
        return c

    lax.fori_loop(0, tm, start, 0)
    for k in range(TOP_K):
        pltpu.make_async_copy(pk_scr, xs_ref.at[pl.ds(0, tm), :], sem).wait()


def _dispatch(dest, h2, xs_init):
    n, d = h2.shape
    tm = TOK_TILE
    return pl.pallas_call(
        functools.partial(_dispatch_kernel, tm=tm),
        grid=(n // tm,),
        in_specs=[pl.BlockSpec((TOP_K * tm,), lambda i: (i,), memory_space=pltpu.SMEM),
                  pl.BlockSpec((tm, d), lambda i: (i, 0)),
                  pl.BlockSpec(memory_space=pl.ANY)],
        out_specs=pl.BlockSpec(memory_space=pl.ANY),
        out_shape=jax.ShapeDtypeStruct(xs_init.shape, xs_init.dtype),
        scratch_shapes=[pltpu.VMEM((tm, d // 2), jnp.uint32), pltpu.SemaphoreType.DMA(())],
        input_output_aliases={2: 0},
        compiler_params=_cparams(1),
        name="dispatch",
    )(dest, h2, xs_init)


def _expert_kernel(blk_e_ref, n_used_ref, xs_ref, wi_ref, wo_ref, y_ref, wi_scr, wo_scr):
    i = pl.program_id(0)
    used = i < n_used_ref[0]

    @pl.when(used & ((i == 0) | (blk_e_ref[i] != blk_e_ref[jnp.maximum(i - 1, 0)])))
    def _():
        wi_scr[...] = wi_ref[0, 0].astype(BF16)
        wo_scr[...] = wo_ref[0, 0].astype(BF16)

    @pl.when(used)
    def _():
        x = jnp.concatenate(_unpack_bf16(xs_ref[...]), axis=1).astype(BF16)
        ag = _dot(x, wi_scr[...])
        a = ag[:, :E_HIDDEN]
        g = ag[:, E_HIDDEN:]
        y_ref[...] = _pack_bf16(_dot((a * _sigmoid(a) * g).astype(BF16), wo_scr[...]))

    @pl.when(jnp.logical_not(used))
    def _():
        y_ref[...] = jnp.zeros_like(y_ref)


def _experts(blk_e, n_used, xs, w_e_in, w_e_out, layer):
    n_slots, dp = xs.shape
    d = 2 * dp
    n_blk = n_slots // MOE_ROWS

    def blk(i, be, nu):
        return (jnp.minimum(i, nu[0] - 1), 0)

    def wsel(i, be, nu):
        return (layer, be[jnp.minimum(i, nu[0] - 1)], 0, 0)

    return pl.pallas_call(
        _expert_kernel,
        grid_spec=pltpu.PrefetchScalarGridSpec(
            num_scalar_prefetch=2,
            grid=(n_blk,),
            in_specs=[pl.BlockSpec((MOE_ROWS, dp), blk),
                      pl.BlockSpec((1, 1, d, 2 * E_HIDDEN), wsel),
                      pl.BlockSpec((1, 1, E_HIDDEN, d), wsel)],
            out_specs=pl.BlockSpec((MOE_ROWS, dp), lambda i, be, nu: (i, 0)),
            scratch_shapes=[pltpu.VMEM((d, 2 * E_HIDDEN), BF16), pltpu.VMEM((E_HIDDEN, d), BF16)],
        ),
        out_shape=jax.ShapeDtypeStruct((n_slots, dp), jnp.uint32),
        compiler_params=_cparams(1),
        name="experts",
    )(blk_e, n_used, xs, w_e_in, w_e_out)


def _combine_kernel(dest_ref, y_ref, h_ref, x1_ref, wn_ref, ml_ref, mc_ref, wsi_ref, wso_ref, ln_ref,
                    x2_ref, g_scr, sem, *, tm, n_lat, n_tok, alpha):
    def start(r, c):
        for k in range(TOP_K):
            pltpu.make_async_copy(y_ref.at[pl.ds(dest_ref[k * tm + r], 1), :], g_scr.at[k, pl.ds(r, 1), :], sem).start()
        return c

    lax.fori_loop(0, tm, start, 0)
    h = h_ref[...]
    ag = _dot(h.astype(BF16), wsi_ref[...])
    a = ag[:, :S_HIDDEN]
    g = ag[:, S_HIDDEN:]
    f = _dot((a * _sigmoid(a) * g).astype(BF16), wso_ref[...])
    wn = wn_ref[...]
    f_hi = f[:, :D_MODEL // 2]
    f_lo = f[:, D_MODEL // 2:]
    for k in range(TOP_K):
        pltpu.make_async_copy(y_ref.at[pl.ds(0, tm), :], g_scr.at[k], sem).wait()
    for k in range(TOP_K):
        y_hi, y_lo = _unpack_bf16(g_scr[k])
        f_hi = f_hi + wn[:, k:k + 1] * y_hi
        f_lo = f_lo + wn[:, k:k + 1] * y_lo
    f = jnp.concatenate([f_hi, f_lo], axis=1)
    rows = (pl.program_id(0) * tm) % n_tok + lax.broadcasted_iota(jnp.int32, (tm, 1), 0)
    is_ctx = rows >= n_lat
    u = alpha * x1_ref[...] + _pick_mod(is_ctx, mc_ref, ml_ref, 5) * f
    x2_ref[...] = _layer_norm(u, ln_ref[0:1, :], ln_ref[1:2, :])


def _combine(dest, y, h2, x1, wn, mod_lat, mod_ctx, ws_in, ws_out, ln2, *, n_lat, n_tok, alpha):
    n, d = h2.shape
    tm = TOK_TILE
    tiles_per_batch = n_tok // tm
    row = lambda i: (i, 0)
    const = lambda i: (0, 0)
    return pl.pallas_call(
        functools.partial(_combine_kernel, tm=tm, n_lat=n_lat, n_tok=n_tok, alpha=alpha),
        grid=(n // tm,),
        in_specs=[pl.BlockSpec((TOP_K * tm,), lambda i: (i,), memory_space=pltpu.SMEM),
                  pl.BlockSpec(memory_space=pl.ANY),
                  pl.BlockSpec((tm, d), row),
                  pl.BlockSpec((tm, d), row),
                  pl.BlockSpec((tm, 128), row),
                  pl.BlockSpec((1, 8, d), lambda i: (i // tiles_per_batch, 0, 0)),
                  pl.BlockSpec((8, d), const),
                  pl.BlockSpec((d, 2 * S_HIDDEN), const),
                  pl.BlockSpec((S_HIDDEN, d), const),
                  pl.BlockSpec((8, d), const)],
        out_specs=pl.BlockSpec((tm, d), row),
        out_shape=jax.ShapeDtypeStruct((n, d), F32),
        scratch_shapes=[pltpu.VMEM((TOP_K, tm, d // 2), jnp.uint32), pltpu.SemaphoreType.DMA(())],
        compiler_params=_cparams(1),
        name="combine",
    )(dest, y, h2, x1, wn, mod_lat, mod_ctx, ws_in, ws_out, ln2)


def _rope_tables(n_lat, n_ctx):
    rows = n_lat // GRID_W
    row = jnp.repeat(jnp.arange(rows, dtype=F32), GRID_W)
    col = jnp.tile(jnp.arange(GRID_W, dtype=F32), rows)
    n_freq = A_DIM // 4
    inv = jnp.power(ROPE_THETA, -jnp.arange(n_freq, dtype=F32) / n_freq)
    ang_r = row[:, None] * inv
    ang_c = col[:, None] * inv
    cos = jnp.concatenate([jnp.cos(ang_r), jnp.cos(ang_r), jnp.cos(ang_c), jnp.cos(ang_c)], axis=1)
    sin = jnp.concatenate([-jnp.sin(ang_r), jnp.sin(ang_r), -jnp.sin(ang_c), jnp.sin(ang_c)], axis=1)
    cos = jnp.concatenate([jnp.tile(cos, (1, 2)), jnp.ones((n_ctx, 128), F32)], axis=0)
    sin = jnp.concatenate([jnp.tile(sin, (1, 2)), jnp.zeros((n_ctx, 128), F32)], axis=0)
    return cos, sin


def _row_tile(n_tok):
    for tm in (1280, 640, 256):
        if n_tok % tm == 0:
            return tm
    raise ValueError(f"unsupported token count {n_tok}")


def _pad_rows(a, rows):
    return jnp.concatenate([a, jnp.zeros((rows - a.shape[0],) + a.shape[1:], a.dtype)], axis=0)


def kernel(x, c, ctx, c_ctx, ada_w, ada_b, w_in, b_gates, conv_qk, lam, subln_a, norm_m, w_br_a, w_br_m, w_out,
           ln1_g, ln1_b, ln2_g, ln2_b, w_router, e_bias, w_e_in, w_e_out, ws_in, ws_out):
    bsz, n_lat, d = x.shape
    n_ctx = ctx.shape[1]
    n_tok = n_lat + n_ctx
    depth = ada_w.shape[0]
    assert d == D_MODEL and n_ctx == MLSTM_CHUNK and n_lat % 512 == 0 and bsz + 1 <= 8
    alpha = (2 * depth) ** 0.25
    tm = _row_tile(n_tok)
    tq = 512
    ck = next(c for c in (1280, 256) if n_tok % c == 0)
    n_all = bsz * n_tok
    n_assign = n_all * TOP_K
    n_blk = -(-n_assign // MOE_ROWS) + N_EXPERTS
    n_slots = n_blk * MOE_ROWS

    cond = _pad_rows(jnp.concatenate([c, c_ctx[None, :]], axis=0), 8)
    mods = _ada(cond, ada_w, ada_b).reshape(depth, 8, N_MOD, d)
    cos_t, sin_t = _rope_tables(n_lat, n_ctx)
    xc = jnp.concatenate([x, ctx], axis=1)
    g_lo = 6 * COL_TILE
    g_hi = g_lo + 4 * M_HEADS

    for l in range(depth):
        lam_init = 0.8 - 0.6 * math.exp(-0.3 * l)
        mod_lat = _pad_rows(mods[l, :bsz].transpose(1, 0, 2), 8).transpose(1, 0, 2)
        mod_ctx = _pad_rows(mods[l, bsz], 8)
        w_main = jnp.concatenate([w_in[l, :, :g_lo], w_in[l, :, g_hi:]], axis=1).astype(BF16)
        w_gates = jnp.pad(w_in[l, :, g_lo:g_hi], ((0, 0), (0, 128 - 4 * M_HEADS)))
        bg = jnp.pad(b_gates[l], (0, 128 - 4 * M_HEADS))[None, :]
        conv_w = _pad_rows(conv_qk[l], 8)
        lam_p = jnp.pad(lam[l], ((0, 4), (0, 128 - A_DIM)))
        subln = subln_a[l][None, :]

        z, gates = _proj(xc, mod_lat, mod_ctx, w_main, w_gates, bg, cos_t, sin_t, conv_w, n_lat=n_lat, tm=tm)
        oa_lat = _attention(z, lam_p, subln, n_lat=n_lat, tq=tq, ck=ck, lat_queries=True, lam_init=lam_init)
        oa_ctx = _attention(z, lam_p, subln, n_lat=n_lat, tq=n_ctx, ck=ck, lat_queries=False, lam_init=lam_init)
        gates_t = gates[:, :, :4 * M_HEADS].transpose(0, 2, 1)
        hf, hb = _mlstm(z, gates, gates_t)
        ln1 = _pad_rows(jnp.stack([ln1_g[l], ln1_b[l]]), 8)
        x1, h2 = _merge(xc, oa_lat, oa_ctx, hf, hb, z, mod_lat, mod_ctx, norm_m[l][None, :], w_br_a[l].astype(BF16),
                        w_br_m[l].astype(BF16), w_out[l].astype(BF16), ln1, n_lat=n_lat, tm=TOK_TILE, alpha=alpha)

        h2f = h2.reshape(n_all, d)
        idx, wn, pos, cnt = _router(h2f, w_router[l], e_bias[l][None, :])
        counts = cnt[0]
        padded = (counts + MOE_ROWS - 1) // MOE_ROWS * MOE_ROWS
        pend = jnp.cumsum(padded)
        pstart = pend - padded
        dest = _slots(idx, pos, jnp.broadcast_to(pstart.astype(F32)[None, :], (8, N_EXPERTS)))
        blk_row = jnp.arange(n_blk, dtype=jnp.int32) * MOE_ROWS
        blk_e = jnp.minimum(jnp.sum(pend[None, :] <= blk_row[:, None], axis=1), N_EXPERTS - 1).astype(jnp.int32)
        n_used = (pend[-1:] // MOE_ROWS).astype(jnp.int32)
        xs = _dispatch(dest, h2f, jnp.zeros((n_slots, d // 2), jnp.uint32))
        y = _experts(blk_e, n_used, xs, w_e_in, w_e_out, l)
        ln2 = _pad_rows(jnp.stack([ln2_g[l], ln2_b[l]]), 8)
        x2 = _combine(dest, y, h2f, x1.reshape(n_all, d), wn, mod_lat, mod_ctx, ws_in[l].astype(BF16),
                      ws_out[l].astype(BF16), ln2, n_lat=n_lat, n_tok=n_tok, alpha=alpha)
        xc = x2.reshape(bsz, n_tok, d)
    return xc[:, :n_lat]
```

```python
import functools
import math

import jax
import jax.numpy as jnp
from jax import lax
from jax.experimental import pallas as pl
from jax.experimental.pallas import tpu as pltpu

F32 = jnp.float32
BF16 = jnp.bfloat16

D_MODEL = 1024
GRID_W = 64
A_HEADS = 8
A_DIM = 64
A_VDIM = 2 * A_DIM
ROPE_THETA = 10000.0
Q_SCALE = A_DIM ** -0.5 * math.log2(math.e)
M_HEADS = 4
M_QK = 128
M_V = 256
M_VAUG = M_V + 128
N_EXPERTS = 256
TOP_K = 8
E_HIDDEN = 256
S_HIDDEN = 256
ROUTE_SCALE = 2.5
LN_EPS = 1e-5
RMS_EPS = 1e-6
N_MOD = 6
COL_TILE = 1024
N_COL_TILES = 8
PROJ_GROUP = 512
ATTN_UNROLL = 12
ATTN_TILES = 1
MLSTM_CHUNK = 256
MOE_ROWS = 512
TOK_TILE = 256
VMEM_LIMIT = 56 * 1024 * 1024


def _cparams(n_axes):
    return pltpu.CompilerParams(dimension_semantics=("arbitrary",) * n_axes,
                                vmem_limit_bytes=VMEM_LIMIT)


def _dot(a, b):
    return jnp.dot(a, b, preferred_element_type=F32)


def _dot_nt(a, b):
    return lax.dot_general(a, b, (((1,), (1,)), ((), ())), preferred_element_type=F32)


def _split_bf16(a):
    hi = a.astype(BF16)
    lo = (a - hi.astype(F32)).astype(BF16)
    return hi, lo


def _dot3(a, b):
    ah, al = _split_bf16(a)
    bh, bl = _split_bf16(b)
    return _dot(ah, bh) + _dot(al, bh) + _dot(ah, bl)


def _pack_bf16(x):
    n = x.shape[1] // 2
    hi = lax.bitcast_convert_type(x[:, :n].astype(BF16).astype(F32), jnp.uint32)
    lo = lax.bitcast_convert_type(x[:, n:].astype(BF16).astype(F32), jnp.uint32)
    return hi | (lo >> 16)


def _unpack_bf16(p):
    hi = lax.bitcast_convert_type(p & jnp.uint32(0xFFFF0000), F32)
    lo = lax.bitcast_convert_type(p << 16, F32)
    return hi, lo


def _sigmoid(v):
    return 1.0 / (1.0 + jnp.exp(-v))


def _log_sigmoid(v):
    return jnp.minimum(v, 0.0) - jnp.log(1.0 + jnp.exp(-jnp.abs(v)))


def _layer_norm(u, g, b):
    mu = jnp.mean(u, axis=-1, keepdims=True)
    var = jnp.mean(jnp.square(u - mu), axis=-1, keepdims=True)
    return (u - mu) * lax.rsqrt(var + LN_EPS) * g + b


def _pick_mod(is_ctx, mc_ref, ml_ref, k):
    return jnp.where(is_ctx, mc_ref[k:k + 1, :], ml_ref[0, k:k + 1, :])


def _ada_kernel(cond_ref, w_ref, b_ref, o_ref):
    c = cond_ref[...]
    c = c * _sigmoid(c)
    o_ref[0] = _dot3(c, w_ref[0]) + b_ref[0]


def _ada(cond, ada_w, ada_b):
    depth, d, width = ada_w.shape
    tn = 512
    return pl.pallas_call(
        _ada_kernel,
        grid=(depth, width // tn),
        in_specs=[pl.BlockSpec((8, d), lambda l, j: (0, 0)),
                  pl.BlockSpec((1, d, tn), lambda l, j: (l, 0, j)),
                  pl.BlockSpec((1, 1, tn), lambda l, j: (l, 0, j))],
        out_specs=pl.BlockSpec((1, 8, tn), lambda l, j: (l, 0, j)),
        out_shape=jax.ShapeDtypeStruct((depth, 8, width), F32),
        compiler_params=_cparams(2),
        name="ada",
    )(cond, ada_w, ada_b.reshape(depth, 1, width))


def _proj_kernel(x_ref, xp_ref, xn_ref, ml_ref, mc_ref, w_ref, wg_ref, bg_ref, cos_ref, sin_ref,
                 conv_ref, z_ref, gates_ref, h_scr, halo_scr, *, tm, n_lat, n_tok):
    i = pl.program_id(1)
    j = pl.program_id(2)
    row0 = i * tm

    def modulate(xv, rows):
        is_ctx = rows >= n_lat
        return xv * (1.0 + _pick_mod(is_ctx, mc_ref, ml_ref, 1)) + _pick_mod(is_ctx, mc_ref, ml_ref, 0)

    @pl.when(j == 0)
    def _():
        rows = row0 + lax.broadcasted_iota(jnp.int32, (tm, 1), 0)
        h = modulate(x_ref[0], rows)
        h_scr[...] = h.astype(BF16)
        r8 = lax.broadcasted_iota(jnp.int32, (8, 1), 0)
        halo_scr[0:8, :] = modulate(xp_ref[0], row0 - 8 + r8)
        halo_scr[8:16, :] = modulate(xn_ref[0], row0 + tm + r8)
        gates_ref[0] = _dot3(h, wg_ref[...]) + bg_ref[...]

    def col_groups(fn):
        for lo in range(0, COL_TILE, PROJ_GROUP):
            fn(_dot(h_scr[...], w_ref[:, lo:lo + PROJ_GROUP]), lo)

    @pl.when(j <= 1)
    def _():
        lane = lax.broadcasted_iota(jnp.int32, (tm, 128), 1)
        first = (lane % 32) < 16
        scale = jnp.where(j == 0, Q_SCALE, 1.0).astype(F32)
        cos = cos_ref[...] * scale
        sin = sin_ref[...] * scale

        def rope(zt, lo):
            for g in range(PROJ_GROUP // 128):
                zg = zt[:, g * 128:(g + 1) * 128]
                partner = jnp.where(first, pltpu.roll(zg, 128 - 16, 1), pltpu.roll(zg, 16, 1))
                z_ref[0, :, lo + g * 128:lo + (g + 1) * 128] = (zg * cos + partner * sin).astype(BF16)

        col_groups(rope)

    @pl.when(j == 3)
    def _():
        loc = lax.broadcasted_iota(jnp.int32, (tm, 1), 0)
        rows = row0 + loc
        seg_start = (rows == 0) | (rows == n_lat)
        seg_end = (rows == n_lat - 1) | (rows == n_tok - 1)
        halo = halo_scr[...].astype(BF16)

        def conv(zt, lo):
            zh = _dot(halo, w_ref[:, lo:lo + PROJ_GROUP])
            zd = jnp.where(loc == 0, zh[7:8, :], pltpu.roll(zt, 1, 0))
            zd = jnp.where(seg_start, 0.0, zd)
            zu = jnp.where(loc == tm - 1, zh[8:9, :], pltpu.roll(zt, tm - 1, 0))
            zu = jnp.where(seg_end, 0.0, zu)
            cw = conv_ref[:, lo:lo + PROJ_GROUP]
            y = zd * cw[0:1, :] + zt * cw[1:2, :] + zu * cw[2:3, :]
            y = y * _sigmoid(y)
            if lo >= M_HEADS * M_QK:
                y = y * (M_QK ** -0.5)
            z_ref[0, :, lo:lo + PROJ_GROUP] = y.astype(BF16)

        col_groups(conv)

    @pl.when((j == 2) | (j >= 4))
    def _():
        def plain(zt, lo):
            z_ref[0, :, lo:lo + PROJ_GROUP] = zt.astype(BF16)

        col_groups(plain)


def _proj(xc, mod_lat, mod_ctx, w_main, w_gates, b_gates, cos_t, sin_t, conv_w, *, n_lat, tm):
    bsz, n_tok, d = xc.shape
    nt8 = n_tok // 8
    kern = functools.partial(_proj_kernel, tm=tm, n_lat=n_lat, n_tok=n_tok)
    return pl.pallas_call(
        kern,
        grid=(bsz, n_tok // tm, N_COL_TILES),
        in_specs=[
            pl.BlockSpec((1, tm, d), lambda b, i, j: (b, i, 0)),
            pl.BlockSpec((1, 8, d), lambda b, i, j: (b, jnp.maximum(i * (tm // 8) - 1, 0), 0)),
            pl.BlockSpec((1, 8, d), lambda b, i, j: (b, jnp.minimum((i + 1) * (tm // 8), nt8 - 1), 0)),
            pl.BlockSpec((1, 8, d), lambda b, i, j: (b, 0, 0)),
            pl.BlockSpec((8, d), lambda b, i, j: (0, 0)),
            pl.BlockSpec((d, COL_TILE), lambda b, i, j: (0, j)),
            pl.BlockSpec((d, 128), lambda b, i, j: (0, 0)),
            pl.BlockSpec((1, 128), lambda b, i, j: (0, 0)),
            pl.BlockSpec((tm, 128), lambda b, i, j: (i, 0)),
            pl.BlockSpec((tm, 128), lambda b, i, j: (i, 0)),
            pl.BlockSpec((8, COL_TILE), lambda b, i, j: (0, 0)),
        ],
        out_specs=[
            pl.BlockSpec((1, tm, COL_TILE), lambda b, i, j: (b, i, j)),
            pl.BlockSpec((1, tm, 128), lambda b, i, j: (b, i, 0)),
        ],
        out_shape=[jax.ShapeDtypeStruct((bsz, n_tok, N_COL_TILES * COL_TILE), BF16),
                   jax.ShapeDtypeStruct((bsz, n_tok, 128), F32)],
        scratch_shapes=[pltpu.VMEM((tm, d), BF16), pltpu.VMEM((16, d), F32)],
        compiler_params=_cparams(3),
        name="proj",
    )(xc, xc, xc, mod_lat, mod_ctx, w_main, w_gates, b_gates, cos_t, sin_t, conv_w)


def _attn_kernel(lam_ref, g_ref, q_ref, k_ref, v_ref, o_ref, *s_scr, tq, tiles, ck, n_chunks, kv_lo, lam_init):
    lane = lax.broadcasted_iota(jnp.int32, (1, A_VDIM), 1)
    slots = (s_scr[0:2], s_scr[2:4])

    def masked_q(t):
        q = q_ref[0, t * tq:(t + 1) * tq, :]
        zero = jnp.zeros_like(q)
        return jnp.where(lane < A_DIM, q, zero), jnp.where(lane >= A_DIM, q, zero)

    ones_blk = jnp.where(lax.broadcasted_iota(jnp.int32, (ck, A_VDIM), 1) == 0, 1.0, 0.0).astype(BF16)

    def kv_start(c):
        return pl.multiple_of(kv_lo + c * ck, 128)

    def scores(qs, c, slot):
        k = k_ref[0, pl.ds(kv_start(c), ck), :]
        for qj, s_ref in zip(qs, slots[slot]):
            s_ref[...] = _dot_nt(qj, k)

    def absorb(c, slot, carry):
        v_aug = jnp.concatenate([v_ref[0, pl.ds(kv_start(c), ck), :], ones_blk], axis=1)
        stage = []
        for s_ref, (m, acc) in zip(slots[slot], carry):
            s = s_ref[...]
            m_new = jnp.maximum(m, jnp.max(s, axis=-1, keepdims=True))
            stage.append((m_new, jnp.exp2(m - m_new) * acc, jnp.exp2(s - m_new).astype(BF16)))
        pv = _dot(jnp.concatenate([p for _, _, p in stage], axis=0), v_aug)
        return tuple((m_new, scaled + pv[j * tq:(j + 1) * tq]) for j, (m_new, scaled, _) in enumerate(stage))

    lam = lam_ref[...]
    s1 = jnp.sum(lam[0:1, :] * lam[1:2, :], axis=-1, keepdims=True)
    s2 = jnp.sum(lam[2:3, :] * lam[3:4, :], axis=-1, keepdims=True)
    lam_val = jnp.exp(s1) - jnp.exp(s2) + lam_init

    def finalize(t, carry):
        (_, acc0), (_, acc1) = carry
        o = (acc0[:, :A_VDIM] / acc0[:, A_VDIM:A_VDIM + 1]
             - lam_val * (acc1[:, :A_VDIM] / acc1[:, A_VDIM:A_VDIM + 1]))
        y = o * lax.rsqrt(jnp.mean(o * o, axis=-1, keepdims=True) + RMS_EPS) * g_ref[...] * (1.0 - lam_init)
        o_ref[0, t * tq:(t + 1) * tq, :] = y.astype(BF16)

    one = (jnp.full((tq, 1), -jnp.inf, F32), jnp.zeros((tq, 2 * A_VDIM), F32))
    qs = masked_q(0)
    scores(qs, 0, 0)
    for t in range(tiles):
        base = t * n_chunks

        def run(c0, count, carry, qs=qs, base=base):
            for u in range(count):
                scores(qs, c0 + u + 1, (base + u + 1) % 2)
                carry = absorb(c0 + u, (base + u) % 2, carry)
            return carry

        trips = (n_chunks - 1) // ATTN_UNROLL
        carry = (one, one)
        if trips > 0:
            carry = lax.fori_loop(0, trips, lambda i, cr, run=run: run(i * ATTN_UNROLL, ATTN_UNROLL, cr), carry)
        rest = n_chunks - 1 - trips * ATTN_UNROLL
        carry = run(trips * ATTN_UNROLL, rest, carry)
        if t + 1 < tiles:
            qs = masked_q(t + 1)
            scores(qs, 0, (base + n_chunks) % 2)
        carry = absorb(n_chunks - 1, (base + n_chunks - 1) % 2, carry)
        finalize(t, carry)


def _attention(z, lam_p, subln, *, n_lat, tq, ck, lat_queries, lam_init):
    bsz, n_tok, _ = z.shape
    n_ctx = n_tok - n_lat
    n_q = n_lat if lat_queries else n_ctx
    if not lat_queries:
        ck = n_ctx
    tiles = ATTN_TILES if n_q % (tq * ATTN_TILES) == 0 else 1
    rows = tq * tiles
    qb = 0 if lat_queries else n_lat // rows
    kern = functools.partial(_attn_kernel, tq=tq, tiles=tiles, ck=ck, n_chunks=(n_tok if lat_queries else n_ctx) // ck,
                             kv_lo=0 if lat_queries else n_lat, lam_init=lam_init)
    return pl.pallas_call(
        kern,
        grid=(bsz, A_HEADS, n_q // rows),
        in_specs=[
            pl.BlockSpec((8, 128), lambda b, h, i: (0, 0)),
            pl.BlockSpec((1, 128), lambda b, h, i: (0, 0)),
            pl.BlockSpec((1, rows, A_VDIM), lambda b, h, i: (b, qb + i, h)),
            pl.BlockSpec((1, n_tok, A_VDIM), lambda b, h, i: (b, 0, A_HEADS + h)),
            pl.BlockSpec((1, n_tok, A_VDIM), lambda b, h, i: (b, 0, 2 * A_HEADS + h)),
        ],
        out_specs=pl.BlockSpec((1, rows, A_VDIM), lambda b, h, i: (b, i, h)),
        out_shape=jax.ShapeDtypeStruct((bsz, n_q, A_HEADS * A_VDIM), BF16),
        scratch_shapes=[pltpu.VMEM((tq, ck), F32)] * 4,
        compiler_params=_cparams(3),
        name="attn_lat" if lat_queries else "attn_ctx",
    )(lam_p, subln, z, z, z)


def _mlstm_kernel(qf_ref, kf_ref, vf_ref, gf_ref, gtf_ref, qb_ref, kb_ref, vb_ref, gb_ref, gtb_ref,
                  hf_ref, hb_ref, c_scr, m_scr, *, t):
    step = pl.program_id(1)

    @pl.when(step == 0)
    def _():
        c_scr[...] = jnp.zeros_like(c_scr)
        m_scr[...] = jnp.zeros_like(m_scr)

    ti = lax.broadcasted_iota(jnp.int32, (t, t), 0)
    si = lax.broadcasted_iota(jnp.int32, (t, t), 1)
    lane = lax.broadcasted_iota(jnp.int32, (t, 128), 1)
    ones_blk = jnp.where(lane == 0, 1.0, 0.0).astype(BF16)
    dirs = ((qf_ref, kf_ref, vf_ref, gf_ref, gtf_ref, hf_ref), (qb_ref, kb_ref, vb_ref, gb_ref, gtb_ref, hb_ref))
    units = []
    for d, (q_ref, k_ref, v_ref, g_ref, gt_ref, h_ref) in enumerate(dirs):
        seen = (si <= ti) if d == 0 else (si >= ti)
        seen_t = (ti <= si) if d == 0 else (ti >= si)
        lf_cols = _log_sigmoid(g_ref[0])
        lf_rows = _log_sigmoid(gt_ref[0])
        for hd in range(M_HEADS):
            u = dict(idx=d * M_HEADS + hd, h_ref=h_ref, hd=hd)
            u["q"] = q_ref[0, :, hd * M_QK:(hd + 1) * M_QK]
            u["k"] = k_ref[0, :, hd * M_QK:(hd + 1) * M_QK]
            v = v_ref[0, :, hd * M_V:(hd + 1) * M_V]
            u["v_aug"] = jnp.concatenate([v, ones_blk], axis=1)
            li = d * 2 * M_HEADS + hd
            lf = li + M_HEADS
            i_col = g_ref[0, :, li:li + 1]
            i_row = gt_ref[0, li:li + 1, :]
            lf_col = lf_cols[:, lf:lf + 1]
            lf_row = lf_rows[lf:lf + 1, :]
            b_col = jnp.sum(jnp.where(seen, lf_row, 0.0), axis=1, keepdims=True)
            b_row = jnp.sum(jnp.where(seen_t, lf_col, 0.0), axis=0, keepdims=True)
            b_end = jnp.sum(lf_row, axis=1, keepdims=True)
            m_old = m_scr[u["idx"], 0:1, 0:1]
            dmat = jnp.where(seen, b_col - b_row + i_row, -jnp.inf)
            u["m_t"] = jnp.maximum(b_col + m_old, jnp.max(dmat, axis=1, keepdims=True))
            u["dexp"] = jnp.exp(dmat - u["m_t"])
            u["iscale"] = jnp.exp(b_col + m_old - u["m_t"])
            g_col = b_end - b_col + i_col
            u["m_new"] = jnp.maximum(b_end + m_old, jnp.max(g_col, axis=0, keepdims=True))
            u["w_col"] = jnp.exp(g_col - u["m_new"])
            u["decay"] = jnp.exp(b_end + m_old - u["m_new"])
            units.append(u)
    for u in units:
        u["qk"] = _dot_nt(u["q"], u["k"])
        u["c_aug"] = c_scr[u["idx"]]
        u["inter"] = _dot(u["q"], u["c_aug"].astype(BF16))
    for u in units:
        wv = (u["w_col"] * u["v_aug"].astype(F32)).astype(BF16)
        k_t = u["k"].astype(F32).T.astype(BF16)
        c_scr[u["idx"]] = u["decay"] * u["c_aug"] + _dot(k_t, wv)
        m_scr[u["idx"]] = jnp.broadcast_to(u["m_new"], (8, 128))
    for u in units:
        intra = _dot((u["qk"] * u["dexp"]).astype(BF16), u["v_aug"])
        num = u["iscale"] * u["inter"][:, :M_V] + intra[:, :M_V]
        den = u["iscale"] * u["inter"][:, M_V:M_V + 1] + intra[:, M_V:M_V + 1]
        hd = u["hd"]
        u["h_ref"][0, :, hd * M_V:(hd + 1) * M_V] = num / jnp.maximum(jnp.abs(den), jnp.exp(-u["m_t"]))


def _mlstm(z, gates, gates_t):
    bsz, n_tok, _ = z.shape
    t = MLSTM_CHUNK
    nc = n_tok // t
    qcol = 3 * COL_TILE // (M_HEADS * M_QK)

    def cf(s):
        return jnp.where(s == 0, nc - 1, s - 1)

    def cb(s):
        return nc - 1 - s

    def specs(c):
        return [
            pl.BlockSpec((1, t, M_HEADS * M_QK), lambda b, s: (b, c(s), qcol)),
            pl.BlockSpec((1, t, M_HEADS * M_QK), lambda b, s: (b, c(s), qcol + 1)),
            pl.BlockSpec((1, t, M_HEADS * M_V), lambda b, s: (b, c(s), 4)),
            pl.BlockSpec((1, t, 128), lambda b, s: (b, c(s), 0)),
            pl.BlockSpec((1, 16, t), lambda b, s: (b, 0, c(s))),
        ]

    out_sd = jax.ShapeDtypeStruct((bsz, n_tok, M_HEADS * M_V), F32)
    return pl.pallas_call(
        functools.partial(_mlstm_kernel, t=t),
        grid=(bsz, nc),
        in_specs=specs(cf) + specs(cb),
        out_specs=[pl.BlockSpec((1, t, M_HEADS * M_V), lambda b, s: (b, cf(s), 0)),
                   pl.BlockSpec((1, t, M_HEADS * M_V), lambda b, s: (b, cb(s), 0))],
        out_shape=[out_sd, out_sd],
        scratch_shapes=[pltpu.VMEM((2 * M_HEADS, M_QK, M_VAUG), F32), pltpu.VMEM((2 * M_HEADS, 8, 128), F32)],
        compiler_params=_cparams(2),
        name="mlstm",
    )(z, z, z, gates, gates_t, z, z, z, gates, gates_t)


def _merge_kernel(x_ref, oal_ref, oac_ref, hf_ref, hb_ref, om_ref, ga_ref, gb_ref, ml_ref, mc_ref, nm_ref,
                  wa_ref, wm_ref, wo_ref, ln_ref, x1_ref, h2_ref, *, tm, n_lat, alpha):
    i = pl.program_id(1)
    rows = i * tm + lax.broadcasted_iota(jnp.int32, (tm, 1), 0)
    is_ctx = rows >= n_lat
    oa = jnp.where(i * tm >= n_lat, oac_ref[0], oal_ref[0])
    hm = hf_ref[0] + hb_ref[0]
    parts = []
    for hd in range(M_HEADS):
        seg = hm[:, hd * M_V:(hd + 1) * M_V]
        parts.append(seg * lax.rsqrt(jnp.mean(seg * seg, axis=-1, keepdims=True) + RMS_EPS))
    hm = jnp.concatenate(parts, axis=1) * nm_ref[...] * _sigmoid(om_ref[0].astype(F32))
    ya = _dot(oa, wa_ref[...])
    ym = _dot(hm.astype(BF16), wm_ref[...])
    y = _sigmoid(ga_ref[0].astype(F32)) * ya + _sigmoid(gb_ref[0].astype(F32)) * ym
    y = _dot(y.astype(BF16), wo_ref[...])
    u = alpha * x_ref[0] + _pick_mod(is_ctx, mc_ref, ml_ref, 2) * y
    x1 = _layer_norm(u, ln_ref[0:1, :], ln_ref[1:2, :])
    x1_ref[0] = x1
    h2_ref[0] = x1 * (1.0 + _pick_mod(is_ctx, mc_ref, ml_ref, 4)) + _pick_mod(is_ctx, mc_ref, ml_ref, 3)


def _merge(xc, oa_lat, oa_ctx, hf, hb, z, mod_lat, mod_ctx, norm_m, w_a, w_m, w_o, ln1, *, n_lat, tm, alpha):
    bsz, n_tok, d = xc.shape
    row = lambda b, i: (b, i, 0)
    const = lambda b, i: (0, 0)
    out_sd = jax.ShapeDtypeStruct((bsz, n_tok, d), F32)
    lat_tiles = n_lat // tm
    return pl.pallas_call(
        functools.partial(_merge_kernel, tm=tm, n_lat=n_lat, alpha=alpha),
        grid=(bsz, n_tok // tm),
        in_specs=[
            pl.BlockSpec((1, tm, d), row),
            pl.BlockSpec((1, tm, d), lambda b, i: (b, jnp.minimum(i, lat_tiles - 1), 0)),
            pl.BlockSpec((1, tm, d), lambda b, i: (b, jnp.maximum(i - lat_tiles, 0), 0)),
            pl.BlockSpec((1, tm, d), row),
            pl.BlockSpec((1, tm, d), row),
            pl.BlockSpec((1, tm, COL_TILE), lambda b, i: (b, i, 5)),
            pl.BlockSpec((1, tm, COL_TILE), lambda b, i: (b, i, 6)),
            pl.BlockSpec((1, tm, COL_TILE), lambda b, i: (b, i, 7)),
            pl.BlockSpec((1, 8, d), lambda b, i: (b, 0, 0)),
            pl.BlockSpec((8, d), const),
            pl.BlockSpec((1, d), const),
            pl.BlockSpec((d, d), const),
            pl.BlockSpec((d, d), const),
            pl.BlockSpec((d, d), const),
            pl.BlockSpec((8, d), const),
        ],
        out_specs=[pl.BlockSpec((1, tm, d), row), pl.BlockSpec((1, tm, d), row)],
        out_shape=[out_sd, out_sd],
        compiler_params=_cparams(2),
        name="merge",
    )(xc, oa_lat, oa_ctx, hf, hb, z, z, z, mod_lat, mod_ctx, norm_m, w_a, w_m, w_o, ln1)


def _router_kernel(h_ref, wr_ref, eb_ref, idx_ref, wn_ref, pos_ref, cnt_ref, carry_scr, *, tm):
    @pl.when(pl.program_id(0) == 0)
    def _():
        carry_scr[...] = jnp.zeros_like(carry_scr)

    scores = _sigmoid(_dot3(h_ref[...], wr_ref[...]))
    biased = scores + eb_ref[...]
    lane = lax.broadcasted_iota(jnp.int32, (tm, N_EXPERTS), 1).astype(F32)
    hots, sels, ws = [], [], []
    for _ in range(TOP_K):
        mx = jnp.max(biased, axis=-1, keepdims=True)
        sel = jnp.min(jnp.where(biased == mx, lane, float(N_EXPERTS)), axis=-1, keepdims=True)
        hot = lane == sel
        hots.append(hot)
        sels.append(sel)
        ws.append(jnp.sum(jnp.where(hot, scores, 0.0), axis=-1, keepdims=True))
        biased = jnp.where(hot, -jnp.inf, biased)
    wsum = ws[0]
    chosen = hots[0]
    for r in range(1, TOP_K):
        wsum = wsum + ws[r]
        chosen = chosen | hots[r]
    chosen_f = jnp.where(chosen, 1.0, 0.0)
    ti = lax.broadcasted_iota(jnp.int32, (tm, tm), 0)
    si = lax.broadcasted_iota(jnp.int32, (tm, tm), 1)
    before = jnp.where(si < ti, 1.0, 0.0).astype(BF16)
    rank = carry_scr[0:1, :] + _dot(before, chosen_f.astype(BF16))
    out_lane = lax.broadcasted_iota(jnp.int32, (tm, 128), 1)
    idx_o = jnp.zeros((tm, 128), jnp.int32)
    pos_o = jnp.zeros((tm, 128), jnp.int32)
    wn_o = jnp.zeros((tm, 128), F32)
    for r in range(TOP_K):
        pos_r = jnp.sum(jnp.where(hots[r], rank, 0.0), axis=-1, keepdims=True).astype(jnp.int32)
        idx_o = jnp.where(out_lane == r, sels[r].astype(jnp.int32), idx_o)
        pos_o = jnp.where(out_lane == r, pos_r, pos_o)
        wn_o = jnp.where(out_lane == r, ws[r] / wsum * ROUTE_SCALE, wn_o)
    idx_ref[...] = idx_o
    pos_ref[...] = pos_o
    wn_ref[...] = wn_o
    total = carry_scr[0:1, :] + jnp.sum(chosen_f, axis=0, keepdims=True)
    carry_scr[...] = jnp.broadcast_to(total, carry_scr.shape)
    cnt_ref[...] = jnp.broadcast_to(total, cnt_ref.shape).astype(jnp.int32)


def _router(h2, w_router, e_bias):
    n, d = h2.shape
    tm = TOK_TILE
    row = lambda i: (i, 0)
    const = lambda i: (0, 0)
    return pl.pallas_call(
        functools.partial(_router_kernel, tm=tm),
        grid=(n // tm,),
        in_specs=[pl.BlockSpec((tm, d), row), pl.BlockSpec((d, N_EXPERTS), const),
                  pl.BlockSpec((1, N_EXPERTS), const)],
        out_specs=[pl.BlockSpec((tm, 128), row), pl.BlockSpec((tm, 128), row), pl.BlockSpec((tm, 128), row),
                   pl.BlockSpec((8, N_EXPERTS), const)],
        out_shape=[jax.ShapeDtypeStruct((n, 128), jnp.int32), jax.ShapeDtypeStruct((n, 128), F32),
                   jax.ShapeDtypeStruct((n, 128), jnp.int32), jax.ShapeDtypeStruct((8, N_EXPERTS), jnp.int32)],
        scratch_shapes=[pltpu.VMEM((8, N_EXPERTS), F32)],
        compiler_params=_cparams(1),
        name="router",
    )(h2, w_router, e_bias)


def _slots_kernel(idx_ref, pos_ref, start_ref, dest_ref, *, tm):
    lane = lax.broadcasted_iota(jnp.int32, (tm, N_EXPERTS), 1)
    out_lane = lax.broadcasted_iota(jnp.int32, (tm, 128), 1)
    idx = idx_ref[...]
    pos = pos_ref[...]
    start = start_ref[0:1, :]
    dest = jnp.zeros((tm, 128), F32)
    for r in range(TOP_K):
        base = jnp.sum(jnp.where(lane == idx[:, r:r + 1], start, 0.0), axis=-1, keepdims=True)
        dest = jnp.where(out_lane == r, base + pos[:, r:r + 1].astype(F32), dest)
    dest_ref[0] = dest.T[0:TOP_K, :].astype(jnp.int32)


def _slots(idx, pos, start):
    n = idx.shape[0]
    tm = TOK_TILE
    return pl.pallas_call(
        functools.partial(_slots_kernel, tm=tm),
        grid=(n // tm,),
        in_specs=[pl.BlockSpec((tm, 128), lambda i: (i, 0)), pl.BlockSpec((tm, 128), lambda i: (i, 0)),
                  pl.BlockSpec((8, N_EXPERTS), lambda i: (0, 0))],
        out_specs=pl.BlockSpec((1, TOP_K, tm), lambda i: (i, 0, 0)),
        out_shape=jax.ShapeDtypeStruct((n // tm, TOP_K, tm), jnp.int32),
        compiler_params=_cparams(1),
        name="slots",
    )(idx, pos, start).reshape(-1)


def _dispatch_kernel(dest_ref, h_ref, xs_in_ref, xs_ref, pk_scr, sem, *, tm):
    del xs_in_ref
    pk_scr[...] = _pack_bf16(h_ref[...])

    def start(r, c):
        for k in range(TOP_K):
            pltpu.make_async_copy(pk_scr.at[pl.ds(r, 1), :], xs_ref.at[pl.ds(dest_ref[k * tm + r], 1), :], sem).start()
        return c

    lax.fori_loop(0, tm, start, 0)
    for k in range(TOP_K):
        pltpu.make_async_copy(pk_scr, xs_ref.at[pl.ds(0, tm), :], sem).wait()


def _dispatch(dest, h2, xs_init):
    n, d = h2.shape
    tm = TOK_TILE
    return pl.pallas_call(
        functools.partial(_dispatch_kernel, tm=tm),
        grid=(n // tm,),
        in_specs=[pl.BlockSpec((TOP_K * tm,), lambda i: (i,), memory_space=pltpu.SMEM),
                  pl.BlockSpec((tm, d), lambda i: (i, 0)),
                  pl.BlockSpec(memory_space=pl.ANY)],
        out_specs=pl.BlockSpec(memory_space=pl.ANY),
        out_shape=jax.ShapeDtypeStruct(xs_init.shape, xs_init.dtype),
        scratch_shapes=[pltpu.VMEM((tm, d // 2), jnp.uint32), pltpu.SemaphoreType.DMA(())],
        input_output_aliases={2: 0},
        compiler_params=_cparams(1),
        name="dispatch",
    )(dest, h2, xs_init)


def _expert_kernel(blk_e_ref, n_used_ref, xs_ref, wi_ref, wo_ref, y_ref, wi_scr, wo_scr):
    i = pl.program_id(0)
    used = i < n_used_ref[0]

    @pl.when(used & ((i == 0) | (blk_e_ref[i] != blk_e_ref[jnp.maximum(i - 1, 0)])))
    def _():
        wi_scr[...] = wi_ref[0, 0].astype(BF16)
        wo_scr[...] = wo_ref[0, 0].astype(BF16)

    @pl.when(used)
    def _():
        x = jnp.concatenate(_unpack_bf16(xs_ref[...]), axis=1).astype(BF16)
        ag = _dot(x, wi_scr[...])
        a = ag[:, :E_HIDDEN]
        g = ag[:, E_HIDDEN:]
        y_ref[...] = _pack_bf16(_dot((a * _sigmoid(a) * g).astype(BF16), wo_scr[...]))

    @pl.when(jnp.logical_not(used))
    def _():
        y_ref[...] = jnp.zeros_like(y_ref)


def _experts(blk_e, n_used, xs, w_e_in, w_e_out, layer):
    n_slots, dp = xs.shape
    d = 2 * dp
    n_blk = n_slots // MOE_ROWS

    def blk(i, be, nu):
        return (jnp.minimum(i, nu[0] - 1), 0)

    def wsel(i, be, nu):
        return (layer, be[jnp.minimum(i, nu[0] - 1)], 0, 0)

    return pl.pallas_call(
        _expert_kernel,
        grid_spec=pltpu.PrefetchScalarGridSpec(
            num_scalar_prefetch=2,
            grid=(n_blk,),
            in_specs=[pl.BlockSpec((MOE_ROWS, dp), blk),
                      pl.BlockSpec((1, 1, d, 2 * E_HIDDEN), wsel),
                      pl.BlockSpec((1, 1, E_HIDDEN, d), wsel)],
            out_specs=pl.BlockSpec((MOE_ROWS, dp), lambda i, be, nu: (i, 0)),
            scratch_shapes=[pltpu.VMEM((d, 2 * E_HIDDEN), BF16), pltpu.VMEM((E_HIDDEN, d), BF16)],
        ),
        out_shape=jax.ShapeDtypeStruct((n_slots, dp), jnp.uint32),
        compiler_params=_cparams(1),
        name="experts",
    )(blk_e, n_used, xs, w_e_in, w_e_out)


def _combine_kernel(dest_ref, y_ref, h_ref, x1_ref, wn_ref, ml_ref, mc_ref, wsi_ref, wso_ref, ln_ref,
                    x2_ref, g_scr, sem, *, tm, n_lat, n_tok, alpha):
    def start(r, c):
        for k in range(TOP_K):
            pltpu.make_async_copy(y_ref.at[pl.ds(dest_ref[k * tm + r], 1), :], g_scr.at[k, pl.ds(r, 1), :], sem).start()
        return c

    lax.fori_loop(0, tm, start, 0)
    h = h_ref[...]
    ag = _dot(h.astype(BF16), wsi_ref[...])
    a = ag[:, :S_HIDDEN]
    g = ag[:, S_HIDDEN:]
    f = _dot((a * _sigmoid(a) * g).astype(BF16), wso_ref[...])
    wn = wn_ref[...]
    f_hi = f[:, :D_MODEL // 2]
    f_lo = f[:, D_MODEL // 2:]
    for k in range(TOP_K):
        pltpu.make_async_copy(y_ref.at[pl.ds(0, tm), :], g_scr.at[k], sem).wait()
    for k in range(TOP_K):
        y_hi, y_lo = _unpack_bf16(g_scr[k])
        f_hi = f_hi + wn[:, k:k + 1] * y_hi
        f_lo = f_lo + wn[:, k:k + 1] * y_lo
    f = jnp.concatenate([f_hi, f_lo], axis=1)
    rows = (pl.program_id(0) * tm) % n_tok + lax.broadcasted_iota(jnp.int32, (tm, 1), 0)
    is_ctx = rows >= n_lat
    u = alpha * x1_ref[...] + _pick_mod(is_ctx, mc_ref, ml_ref, 5) * f
    x2_ref[...] = _layer_norm(u, ln_ref[0:1, :], ln_ref[1:2, :])


def _combine(dest, y, h2, x1, wn, mod_lat, mod_ctx, ws_in, ws_out, ln2, *, n_lat, n_tok, alpha):
    n, d = h2.shape
    tm = TOK_TILE
    tiles_per_batch = n_tok // tm
    row = lambda i: (i, 0)
    const = lambda i: (0, 0)
    return pl.pallas_call(
        functools.partial(_combine_kernel, tm=tm, n_lat=n_lat, n_tok=n_tok, alpha=alpha),
        grid=(n // tm,),
        in_specs=[pl.BlockSpec((TOP_K * tm,), lambda i: (i,), memory_space=pltpu.SMEM),
                  pl.BlockSpec(memory_space=pl.ANY),
                  pl.BlockSpec((tm, d), row),
                  pl.BlockSpec((tm, d), row),
                  pl.BlockSpec((tm, 128), row),
                  pl.BlockSpec((1, 8, d), lambda i: (i // tiles_per_batch, 0, 0)),
                  pl.BlockSpec((8, d), const),
                  pl.BlockSpec((d, 2 * S_HIDDEN), const),
                  pl.BlockSpec((S_HIDDEN, d), const),
                  pl.BlockSpec((8, d), const)],
        out_specs=pl.BlockSpec((tm, d), row),
        out_shape=jax.ShapeDtypeStruct((n, d), F32),
        scratch_shapes=[pltpu.VMEM((TOP_K, tm, d // 2), jnp.uint32), pltpu.SemaphoreType.DMA(())],
        compiler_params=_cparams(1),
        name="combine",
    )(dest, y, h2, x1, wn, mod_lat, mod_ctx, ws_in, ws_out, ln2)


def _rope_tables(n_lat, n_ctx):
    rows = n_lat // GRID_W
    row = jnp.repeat(jnp.arange(rows, dtype=F32), GRID_W)
    col = jnp.tile(jnp.arange(GRID_W, dtype=F32), rows)
    n_freq = A_DIM // 4
    inv = jnp.power(ROPE_THETA, -jnp.arange(n_freq, dtype=F32) / n_freq)
    ang_r = row[:, None] * inv
    ang_c = col[:, None] * inv
    cos = jnp.concatenate([jnp.cos(ang_r), jnp.cos(ang_r), jnp.cos(ang_c), jnp.cos(ang_c)], axis=1)
    sin = jnp.concatenate([-jnp.sin(ang_r), jnp.sin(ang_r), -jnp.sin(ang_c), jnp.sin(ang_c)], axis=1)
    cos = jnp.concatenate([jnp.tile(cos, (1, 2)), jnp.ones((n_ctx, 128), F32)], axis=0)
    sin = jnp.concatenate([jnp.tile(sin, (1, 2)), jnp.zeros((n_ctx, 128), F32)], axis=0)
    return cos, sin


def _row_tile(n_tok):
    for tm in (1280, 640, 256):
        if n_tok % tm == 0:
            return tm
    raise ValueError(f"unsupported token count {n_tok}")


def _pad_rows(a, rows):
    return jnp.concatenate([a, jnp.zeros((rows - a.shape[0],) + a.shape[1:], a.dtype)], axis=0)


def kernel(x, c, ctx, c_ctx, ada_w, ada_b, w_in, b_gates, conv_qk, lam, subln_a, norm_m, w_br_a, w_br_m, w_out,
           ln1_g, ln1_b, ln2_g, ln2_b, w_router, e_bias, w_e_in, w_e_out, ws_in, ws_out):
    bsz, n_lat, d = x.shape
    n_ctx = ctx.shape[1]
    n_tok = n_lat + n_ctx
    depth = ada_w.shape[0]
    assert d == D_MODEL and n_ctx == MLSTM_CHUNK and n_lat % 512 == 0 and bsz + 1 <= 8
    alpha = (2 * depth) ** 0.25
    tm = _row_tile(n_tok)
    tq = 512
    ck = next(c for c in (1280, 256) if n_tok % c == 0)
    n_all = bsz * n_tok
    n_assign = n_all * TOP_K
    n_blk = -(-n_assign // MOE_ROWS) + N_EXPERTS
    n_slots = n_blk * MOE_ROWS

    cond = _pad_rows(jnp.concatenate([c, c_ctx[None, :]], axis=0), 8)
    mods = _ada(cond, ada_w, ada_b).reshape(depth, 8, N_MOD, d)
    cos_t, sin_t = _rope_tables(n_lat, n_ctx)
    xc = jnp.concatenate([x, ctx], axis=1)
    g_lo = 6 * COL_TILE
    g_hi = g_lo + 4 * M_HEADS

    for l in range(depth):
        lam_init = 0.8 - 0.6 * math.exp(-0.3 * l)
        mod_lat = _pad_rows(mods[l, :bsz].transpose(1, 0, 2), 8).transpose(1, 0, 2)
        mod_ctx = _pad_rows(mods[l, bsz], 8)
        w_main = jnp.concatenate([w_in[l, :, :g_lo], w_in[l, :, g_hi:]], axis=1).astype(BF16)
        w_gates = jnp.pad(w_in[l, :, g_lo:g_hi], ((0, 0), (0, 128 - 4 * M_HEADS)))
        bg = jnp.pad(b_gates[l], (0, 128 - 4 * M_HEADS))[None, :]
        conv_w = _pad_rows(conv_qk[l], 8)
        lam_p = jnp.pad(lam[l], ((0, 4), (0, 128 - A_DIM)))
        subln = subln_a[l][None, :]

        z, gates = _proj(xc, mod_lat, mod_ctx, w_main, w_gates, bg, cos_t, sin_t, conv_w, n_lat=n_lat, tm=tm)
        oa_lat = _attention(z, lam_p, subln, n_lat=n_lat, tq=tq, ck=ck, lat_queries=True, lam_init=lam_init)
        oa_ctx = _attention(z, lam_p, subln, n_lat=n_lat, tq=n_ctx, ck=ck, lat_queries=False, lam_init=lam_init)
        gates_t = gates[:, :, :4 * M_HEADS].transpose(0, 2, 1)
        hf, hb = _mlstm(z, gates, gates_t)
        ln1 = _pad_rows(jnp.stack([ln1_g[l], ln1_b[l]]), 8)
        x1, h2 = _merge(xc, oa_lat, oa_ctx, hf, hb, z, mod_lat, mod_ctx, norm_m[l][None, :], w_br_a[l].astype(BF16),
                        w_br_m[l].astype(BF16), w_out[l].astype(BF16), ln1, n_lat=n_lat, tm=TOK_TILE, alpha=alpha)

        h2f = h2.reshape(n_all, d)
        idx, wn, pos, cnt = _router(h2f, w_router[l], e_bias[l][None, :])
        counts = cnt[0]
        padded = (counts + MOE_ROWS - 1) // MOE_ROWS * MOE_ROWS
        pend = jnp.cumsum(padded)
        pstart = pend - padded
        dest = _slots(idx, pos, jnp.broadcast_to(pstart.astype(F32)[None, :], (8, N_EXPERTS)))
        blk_row = jnp.arange(n_blk, dtype=jnp.int32) * MOE_ROWS
        blk_e = jnp.minimum(jnp.sum(pend[None, :] <= blk_row[:, None], axis=1), N_EXPERTS - 1).astype(jnp.int32)
        n_used = (pend[-1:] // MOE_ROWS).astype(jnp.int32)
        xs = _dispatch(dest, h2f, jnp.zeros((n_slots, d // 2), jnp.uint32))
        y = _experts(blk_e, n_used, xs, w_e_in, w_e_out, l)
        ln2 = _pad_rows(jnp.stack([ln2_g[l], ln2_b[l]]), 8)
        x2 = _combine(dest, y, h2f, x1.reshape(n_all, d), wn, mod_lat, mod_ctx, ws_in[l].astype(BF16),
                      ws_out[l].astype(BF16), ln2, n_lat=n_lat, n_tok=n_tok, alpha=alpha)
        xc = x2.reshape(bsz, n_tok, d)
    return xc[:, :n_lat]
```

```python
import functools
import math

import jax
import jax.numpy as jnp
from jax import lax
from jax.experimental import pallas as pl
from jax.experimental.pallas import tpu as pltpu

F32 = jnp.float32
BF16 = jnp.bfloat16

D_MODEL = 1024
GRID_W = 64
A_HEADS = 8
A_DIM = 64
A_VDIM = 2 * A_DIM
ROPE_THETA = 10000.0
Q_SCALE = A_DIM ** -0.5 * math.log2(math.e)
M_HEADS = 4
M_QK = 128
M_V = 256
M_VAUG = M_V + 128
N_EXPERTS = 256
TOP_K = 8
E_HIDDEN = 256
S_HIDDEN = 256
ROUTE_SCALE = 2.5
LN_EPS = 1e-5
RMS_EPS = 1e-6
N_MOD = 6
COL_TILE = 1024
N_COL_TILES = 8
PROJ_GROUP = 512
ATTN_UNROLL = 12
ATTN_TILES = 1
MLSTM_CHUNK = 256
MOE_ROWS = 512
TOK_TILE = 256
VMEM_LIMIT = 56 * 1024 * 1024


def _cparams(n_axes):
    return pltpu.CompilerParams(dimension_semantics=("arbitrary",) * n_axes,
                                vmem_limit_bytes=VMEM_LIMIT)


def _dot(a, b):
    return jnp.dot(a, b, preferred_element_type=F32)


def _dot_nt(a, b):
    return lax.dot_general(a, b, (((1,), (1,)), ((), ())), preferred_element_type=F32)


def _split_bf16(a):
    hi = a.astype(BF16)
    lo = (a - hi.astype(F32)).astype(BF16)
    return hi, lo


def _dot3(a, b):
    ah, al = _split_bf16(a)
    bh, bl = _split_bf16(b)
    return _dot(ah, bh) + _dot(al, bh) + _dot(ah, bl)


def _pack_bf16(x):
    n = x.shape[1] // 2
    hi = lax.bitcast_convert_type(x[:, :n].astype(BF16).astype(F32), jnp.uint32)
    lo = lax.bitcast_convert_type(x[:, n:].astype(BF16).astype(F32), jnp.uint32)
    return hi | (lo >> 16)


def _unpack_bf16(p):
    hi = lax.bitcast_convert_type(p & jnp.uint32(0xFFFF0000), F32)
    lo = lax.bitcast_convert_type(p << 16, F32)
    return hi, lo


def _sigmoid(v):
    return 1.0 / (1.0 + jnp.exp(-v))


def _log_sigmoid(v):
    return jnp.minimum(v, 0.0) - jnp.log(1.0 + jnp.exp(-jnp.abs(v)))


def _layer_norm(u, g, b):
    mu = jnp.mean(u, axis=-1, keepdims=True)
    var = jnp.mean(jnp.square(u - mu), axis=-1, keepdims=True)
    return (u - mu) * lax.rsqrt(var + LN_EPS) * g + b


def _pick_mod(is_ctx, mc_ref, ml_ref, k):
    return jnp.where(is_ctx, mc_ref[k:k + 1, :], ml_ref[0, k:k + 1, :])


def _ada_kernel(cond_ref, w_ref, b_ref, o_ref):
    c = cond_ref[...]
    c = c * _sigmoid(c)
    o_ref[0] = _dot3(c, w_ref[0]) + b_ref[0]


def _ada(cond, ada_w, ada_b):
    depth, d, width = ada_w.shape
    tn = 512
    return pl.pallas_call(
        _ada_kernel,
        grid=(depth, width // tn),
        in_specs=[pl.BlockSpec((8, d), lambda l, j: (0, 0)),
                  pl.BlockSpec((1, d, tn), lambda l, j: (l, 0, j)),
                  pl.BlockSpec((1, 1, tn), lambda l, j: (l, 0, j))],
        out_specs=pl.BlockSpec((1, 8, tn), lambda l, j: (l, 0, j)),
        out_shape=jax.ShapeDtypeStruct((depth, 8, width), F32),
        compiler_params=_cparams(2),
        name="ada",
    )(cond, ada_w, ada_b.reshape(depth, 1, width))


def _proj_kernel(x_ref, xp_ref, xn_ref, ml_ref, mc_ref, w_ref, wg_ref, bg_ref, cos_ref, sin_ref,
                 conv_ref, z_ref, gates_ref, h_scr, halo_scr, *, tm, n_lat, n_tok):
    i = pl.program_id(1)
    j = pl.program_id(2)
    row0 = i * tm

    def modulate(xv, rows):
        is_ctx = rows >= n_lat
        return xv * (1.0 + _pick_mod(is_ctx, mc_ref, ml_ref, 1)) + _pick_mod(is_ctx, mc_ref, ml_ref, 0)

    @pl.when(j == 0)
    def _():
        rows = row0 + lax.broadcasted_iota(jnp.int32, (tm, 1), 0)
        h = modulate(x_ref[0], rows)
        h_scr[...] = h.astype(BF16)
        r8 = lax.broadcasted_iota(jnp.int32, (8, 1), 0)
        halo_scr[0:8, :] = modulate(xp_ref[0], row0 - 8 + r8)
        halo_scr[8:16, :] = modulate(xn_ref[0], row0 + tm + r8)
        gates_ref[0] = _dot3(h, wg_ref[...]) + bg_ref[...]

    def col_groups(fn):
        for lo in range(0, COL_TILE, PROJ_GROUP):
            fn(_dot(h_scr[...], w_ref[:, lo:lo + PROJ_GROUP]), lo)

    @pl.when(j <= 1)
    def _():
        lane = lax.broadcasted_iota(jnp.int32, (tm, 128), 1)
        first = (lane % 32) < 16
        scale = jnp.where(j == 0, Q_SCALE, 1.0).astype(F32)
        cos = cos_ref[...] * scale
        sin = sin_ref[...] * scale

        def rope(zt, lo):
            for g in range(PROJ_GROUP // 128):
                zg = zt[:, g * 128:(g + 1) * 128]
                partner = jnp.where(first, pltpu.roll(zg, 128 - 16, 1), pltpu.roll(zg, 16, 1))
                z_ref[0, :, lo + g * 128:lo + (g + 1) * 128] = (zg * cos + partner * sin).astype(BF16)

        col_groups(rope)

    @pl.when(j == 3)
    def _():
        loc = lax.broadcasted_iota(jnp.int32, (tm, 1), 0)
        rows = row0 + loc
        seg_start = (rows == 0) | (rows == n_lat)
        seg_end = (rows == n_lat - 1) | (rows == n_tok - 1)
        halo = halo_scr[...].astype(BF16)

        def conv(zt, lo):
            zh = _dot(halo, w_ref[:, lo:lo + PROJ_GROUP])
            zd = jnp.where(loc == 0, zh[7:8, :], pltpu.roll(zt, 1, 0))
            zd = jnp.where(seg_start, 0.0, zd)
            zu = jnp.where(loc == tm - 1, zh[8:9, :], pltpu.roll(zt, tm - 1, 0))
            zu = jnp.where(seg_end, 0.0, zu)
            cw = conv_ref[:, lo:lo + PROJ_GROUP]
            y = zd * cw[0:1, :] + zt * cw[1:2, :] + zu * cw[2:3, :]
            y = y * _sigmoid(y)
            if lo >= M_HEADS * M_QK:
                y = y * (M_QK ** -0.5)
            z_ref[0, :, lo:lo + PROJ_GROUP] = y.astype(BF16)

        col_groups(conv)

    @pl.when((j == 2) | (j >= 4))
    def _():
        def plain(zt, lo):
            z_ref[0, :, lo:lo + PROJ_GROUP] = zt.astype(BF16)

        col_groups(plain)


def _proj(xc, mod_lat, mod_ctx, w_main, w_gates, b_gates, cos_t, sin_t, conv_w, *, n_lat, tm):
    bsz, n_tok, d = xc.shape
    nt8 = n_tok // 8
    kern = functools.partial(_proj_kernel, tm=tm, n_lat=n_lat, n_tok=n_tok)
    return pl.pallas_call(
        kern,
        grid=(bsz, n_tok // tm, N_COL_TILES),
        in_specs=[
            pl.BlockSpec((1, tm, d), lambda b, i, j: (b, i, 0)),
            pl.BlockSpec((1, 8, d), lambda b, i, j: (b, jnp.maximum(i * (tm // 8) - 1, 0), 0)),
            pl.BlockSpec((1, 8, d), lambda b, i, j: (b, jnp.minimum((i + 1) * (tm // 8), nt8 - 1), 0)),
            pl.BlockSpec((1, 8, d), lambda b, i, j: (b, 0, 0)),
            pl.BlockSpec((8, d), lambda b, i, j: (0, 0)),
            pl.BlockSpec((d, COL_TILE), lambda b, i, j: (0, j)),
            pl.BlockSpec((d, 128), lambda b, i, j: (0, 0)),
            pl.BlockSpec((1, 128), lambda b, i, j: (0, 0)),
            pl.BlockSpec((tm, 128), lambda b, i, j: (i, 0)),
            pl.BlockSpec((tm, 128), lambda b, i, j: (i, 0)),
            pl.BlockSpec((8, COL_TILE), lambda b, i, j: (0, 0)),
        ],
        out_specs=[
            pl.BlockSpec((1, tm, COL_TILE), lambda b, i, j: (b, i, j)),
            pl.BlockSpec((1, tm, 128), lambda b, i, j: (b, i, 0)),
        ],
        out_shape=[jax.ShapeDtypeStruct((bsz, n_tok, N_COL_TILES * COL_TILE), BF16),
                   jax.ShapeDtypeStruct((bsz, n_tok, 128), F32)],
        scratch_shapes=[pltpu.VMEM((tm, d), BF16), pltpu.VMEM((16, d), F32)],
        compiler_params=_cparams(3),
        name="proj",
    )(xc, xc, xc, mod_lat, mod_ctx, w_main, w_gates, b_gates, cos_t, sin_t, conv_w)


def _attn_kernel(lam_ref, g_ref, q_ref, k_ref, v_ref, o_ref, *s_scr, tq, tiles, ck, n_chunks, kv_lo, lam_init):
    lane = lax.broadcasted_iota(jnp.int32, (1, A_VDIM), 1)
    slots = (s_scr[0:2], s_scr[2:4])

    def masked_q(t):
        q = q_ref[0, t * tq:(t + 1) * tq, :]
        zero = jnp.zeros_like(q)
        return jnp.where(lane < A_DIM, q, zero), jnp.where(lane >= A_DIM, q, zero)

    ones_blk = jnp.where(lax.broadcasted_iota(jnp.int32, (ck, A_VDIM), 1) == 0, 1.0, 0.0).astype(BF16)

    def kv_start(c):
        return pl.multiple_of(kv_lo + c * ck, 128)

    def scores(qs, c, slot):
        k = k_ref[0, pl.ds(kv_start(c), ck), :]
        for qj, s_ref in zip(qs, slots[slot]):
            s_ref[...] = _dot_nt(qj, k)

    def absorb(c, slot, carry):
        v_aug = jnp.concatenate([v_ref[0, pl.ds(kv_start(c), ck), :], ones_blk], axis=1)
        stage = []
        for s_ref, (m, acc) in zip(slots[slot], carry):
            s = s_ref[...]
            m_new = jnp.maximum(m, jnp.max(s, axis=-1, keepdims=True))
            stage.append((m_new, jnp.exp2(m - m_new) * acc, jnp.exp2(s - m_new).astype(BF16)))
        pv = _dot(jnp.concatenate([p for _, _, p in stage], axis=0), v_aug)
        return tuple((m_new, scaled + pv[j * tq:(j + 1) * tq]) for j, (m_new, scaled, _) in enumerate(stage))

    lam = lam_ref[...]
    s1 = jnp.sum(lam[0:1, :] * lam[1:2, :], axis=-1, keepdims=True)
    s2 = jnp.sum(lam[2:3, :] * lam[3:4, :], axis=-1, keepdims=True)
    lam_val = jnp.exp(s1) - jnp.exp(s2) + lam_init

    def finalize(t, carry):
        (_, acc0), (_, acc1) = carry
        o = (acc0[:, :A_VDIM] / acc0[:, A_VDIM:A_VDIM + 1]
             - lam_val * (acc1[:, :A_VDIM] / acc1[:, A_VDIM:A_VDIM + 1]))
        y = o * lax.rsqrt(jnp.mean(o * o, axis=-1, keepdims=True) + RMS_EPS) * g_ref[...] * (1.0 - lam_init)
        o_ref[0, t * tq:(t + 1) * tq, :] = y.astype(BF16)

    one = (jnp.full((tq, 1), -jnp.inf, F32), jnp.zeros((tq, 2 * A_VDIM), F32))
    qs = masked_q(0)
    scores(qs, 0, 0)
    for t in range(tiles):
        base = t * n_chunks

        def run(c0, count, carry, qs=qs, base=base):
            for u in range(count):
                scores(qs, c0 + u + 1, (base + u + 1) % 2)
                carry = absorb(c0 + u, (base + u) % 2, carry)
            return carry

        trips = (n_chunks - 1) // ATTN_UNROLL
        carry = (one, one)
        if trips > 0:
            carry = lax.fori_loop(0, trips, lambda i, cr, run=run: run(i * ATTN_UNROLL, ATTN_UNROLL, cr), carry)
        rest = n_chunks - 1 - trips * ATTN_UNROLL
        carry = run(trips * ATTN_UNROLL, rest, carry)
        if t + 1 < tiles:
            qs = masked_q(t + 1)
            scores(qs, 0, (base + n_chunks) % 2)
        carry = absorb(n_chunks - 1, (base + n_chunks - 1) % 2, carry)
        finalize(t, carry)


def _attention(z, lam_p, subln, *, n_lat, tq, ck, lat_queries, lam_init):
    bsz, n_tok, _ = z.shape
    n_ctx = n_tok - n_lat
    n_q = n_lat if lat_queries else n_ctx
    if not lat_queries:
        ck = n_ctx
    tiles = ATTN_TILES if n_q % (tq * ATTN_TILES) == 0 else 1
    rows = tq * tiles
    qb = 0 if lat_queries else n_lat // rows
    kern = functools.partial(_attn_kernel, tq=tq, tiles=tiles, ck=ck, n_chunks=(n_tok if lat_queries else n_ctx) // ck,
                             kv_lo=0 if lat_queries else n_lat, lam_init=lam_init)
    return pl.pallas_call(
        kern,
        grid=(bsz, A_HEADS, n_q // rows),
        in_specs=[
            pl.BlockSpec((8, 128), lambda b, h, i: (0, 0)),
            pl.BlockSpec((1, 128), lambda b, h, i: (0, 0)),
            pl.BlockSpec((1, rows, A_VDIM), lambda b, h, i: (b, qb + i, h)),
            pl.BlockSpec((1, n_tok, A_VDIM), lambda b, h, i: (b, 0, A_HEADS + h)),
            pl.BlockSpec((1, n_tok, A_VDIM), lambda b, h, i: (b, 0, 2 * A_HEADS + h)),
        ],
        out_specs=pl.BlockSpec((1, rows, A_VDIM), lambda b, h, i: (b, i, h)),
        out_shape=jax.ShapeDtypeStruct((bsz, n_q, A_HEADS * A_VDIM), BF16),
        scratch_shapes=[pltpu.VMEM((tq, ck), F32)] * 4,
        compiler_params=_cparams(3),
        name="attn_lat" if lat_queries else "attn_ctx",
    )(lam_p, subln, z, z, z)


def _mlstm_kernel(qf_ref, kf_ref, vf_ref, gf_ref, gtf_ref, qb_ref, kb_ref, vb_ref, gb_ref, gtb_ref,
                  hf_ref, hb_ref, c_scr, m_scr, *, t):
    step = pl.program_id(1)

    @pl.when(step == 0)
    def _():
        c_scr[...] = jnp.zeros_like(c_scr)
        m_scr[...] = jnp.zeros_like(m_scr)

    ti = lax.broadcasted_iota(jnp.int32, (t, t), 0)
    si = lax.broadcasted_iota(jnp.int32, (t, t), 1)
    lane = lax.broadcasted_iota(jnp.int32, (t, 128), 1)
    ones_blk = jnp.where(lane == 0, 1.0, 0.0).astype(BF16)
    dirs = ((qf_ref, kf_ref, vf_ref, gf_ref, gtf_ref, hf_ref), (qb_ref, kb_ref, vb_ref, gb_ref, gtb_ref, hb_ref))
    units = []
    for d, (q_ref, k_ref, v_ref, g_ref, gt_ref, h_ref) in enumerate(dirs):
        seen = (si <= ti) if d == 0 else (si >= ti)
        seen_t = (ti <= si) if d == 0 else (ti >= si)
        lf_cols = _log_sigmoid(g_ref[0])
        lf_rows = _log_sigmoid(gt_ref[0])
        for hd in range(M_HEADS):
            u = dict(idx=d * M_HEADS + hd, h_ref=h_ref, hd=hd)
            u["q"] = q_ref[0, :, hd * M_QK:(hd + 1) * M_QK]
            u["k"] = k_ref[0, :, hd * M_QK:(hd + 1) * M_QK]
            v = v_ref[0, :, hd * M_V:(hd + 1) * M_V]
            u["v_aug"] = jnp.concatenate([v, ones_blk], axis=1)
            li = d * 2 * M_HEADS + hd
            lf = li + M_HEADS
            i_col = g_ref[0, :, li:li + 1]
            i_row = gt_ref[0, li:li + 1, :]
            lf_col = lf_cols[:, lf:lf + 1]
            lf_row = lf_rows[lf:lf + 1, :]
            b_col = jnp.sum(jnp.where(seen, lf_row, 0.0), axis=1, keepdims=True)
            b_row = jnp.sum(jnp.where(seen_t, lf_col, 0.0), axis=0, keepdims=True)
            b_end = jnp.sum(lf_row, axis=1, keepdims=True)
            m_old = m_scr[u["idx"], 0:1, 0:1]
            dmat = jnp.where(seen, b_col - b_row + i_row, -jnp.inf)
            u["m_t"] = jnp.maximum(b_col + m_old, jnp.max(dmat, axis=1, keepdims=True))
            u["dexp"] = jnp.exp(dmat - u["m_t"])
            u["iscale"] = jnp.exp(b_col + m_old - u["m_t"])
            g_col = b_end - b_col + i_col
            u["m_new"] = jnp.maximum(b_end + m_old, jnp.max(g_col, axis=0, keepdims=True))
            u["w_col"] = jnp.exp(g_col - u["m_new"])
            u["decay"] = jnp.exp(b_end + m_old - u["m_new"])
            units.append(u)
    for u in units:
        u["qk"] = _dot_nt(u["q"], u["k"])
        u["c_aug"] = c_scr[u["idx"]]
        u["inter"] = _dot(u["q"], u["c_aug"].astype(BF16))
    for u in units:
        wv = (u["w_col"] * u["v_aug"].astype(F32)).astype(BF16)
        k_t = u["k"].astype(F32).T.astype(BF16)
        c_scr[u["idx"]] = u["decay"] * u["c_aug"] + _dot(k_t, wv)
        m_scr[u["idx"]] = jnp.broadcast_to(u["m_new"], (8, 128))
    for u in units:
        intra = _dot((u["qk"] * u["dexp"]).astype(BF16), u["v_aug"])
        num = u["iscale"] * u["inter"][:, :M_V] + intra[:, :M_V]
        den = u["iscale"] * u["inter"][:, M_V:M_V + 1] + intra[:, M_V:M_V + 1]
        hd = u["hd"]
        u["h_ref"][0, :, hd * M_V:(hd + 1) * M_V] = num / jnp.maximum(jnp.abs(den), jnp.exp(-u["m_t"]))


def _mlstm(z, gates, gates_t):
    bsz, n_tok, _ = z.shape
    t = MLSTM_CHUNK
    nc = n_tok // t
    qcol = 3 * COL_TILE // (M_HEADS * M_QK)

    def cf(s):
        return jnp.where(s == 0, nc - 1, s - 1)

    def cb(s):
        return nc - 1 - s

    def specs(c):
        return [
            pl.BlockSpec((1, t, M_HEADS * M_QK), lambda b, s: (b, c(s), qcol)),
            pl.BlockSpec((1, t, M_HEADS * M_QK), lambda b, s: (b, c(s), qcol + 1)),
            pl.BlockSpec((1, t, M_HEADS * M_V), lambda b, s: (b, c(s), 4)),
            pl.BlockSpec((1, t, 128), lambda b, s: (b, c(s), 0)),
            pl.BlockSpec((1, 16, t), lambda b, s: (b, 0, c(s))),
        ]

    out_sd = jax.ShapeDtypeStruct((bsz, n_tok, M_HEADS * M_V), F32)
    return pl.pallas_call(
        functools.partial(_mlstm_kernel, t=t),
        grid=(bsz, nc),
        in_specs=specs(cf) + specs(cb),
        out_specs=[pl.BlockSpec((1, t, M_HEADS * M_V), lambda b, s: (b, cf(s), 0)),
                   pl.BlockSpec((1, t, M_HEADS * M_V), lambda b, s: (b, cb(s), 0))],
        out_shape=[out_sd, out_sd],
        scratch_shapes=[pltpu.VMEM((2 * M_HEADS, M_QK, M_VAUG), F32), pltpu.VMEM((2 * M_HEADS, 8, 128), F32)],
        compiler_params=_cparams(2),
        name="mlstm",
    )(z, z, z, gates, gates_t, z, z, z, gates, gates_t)


def _merge_kernel(x_ref, oal_ref, oac_ref, hf_ref, hb_ref, om_ref, ga_ref, gb_ref, ml_ref, mc_ref, nm_ref,
                  wa_ref, wm_ref, wo_ref, ln_ref, x1_ref, h2_ref, *, tm, n_lat, alpha):
    i = pl.program_id(1)
    rows = i * tm + lax.broadcasted_iota(jnp.int32, (tm, 1), 0)
    is_ctx = rows >= n_lat
    oa = jnp.where(i * tm >= n_lat, oac_ref[0], oal_ref[0])
    hm = hf_ref[0] + hb_ref[0]
    parts = []
    for hd in range(M_HEADS):
        seg = hm[:, hd * M_V:(hd + 1) * M_V]
        parts.append(seg * lax.rsqrt(jnp.mean(seg * seg, axis=-1, keepdims=True) + RMS_EPS))
    hm = jnp.concatenate(parts, axis=1) * nm_ref[...] * _sigmoid(om_ref[0].astype(F32))
    ya = _dot(oa, wa_ref[...])
    ym = _dot(hm.astype(BF16), wm_ref[...])
    y = _sigmoid(ga_ref[0].astype(F32)) * ya + _sigmoid(gb_ref[0].astype(F32)) * ym
    y = _dot(y.astype(BF16), wo_ref[...])
    u = alpha * x_ref[0] + _pick_mod(is_ctx, mc_ref, ml_ref, 2) * y
    x1 = _layer_norm(u, ln_ref[0:1, :], ln_ref[1:2, :])
    x1_ref[0] = x1
    h2_ref[0] = x1 * (1.0 + _pick_mod(is_ctx, mc_ref, ml_ref, 4)) + _pick_mod(is_ctx, mc_ref, ml_ref, 3)


def _merge(xc, oa_lat, oa_ctx, hf, hb, z, mod_lat, mod_ctx, norm_m, w_a, w_m, w_o, ln1, *, n_lat, tm, alpha):
    bsz, n_tok, d = xc.shape
    row = lambda b, i: (b, i, 0)
    const = lambda b, i: (0, 0)
    out_sd = jax.ShapeDtypeStruct((bsz, n_tok, d), F32)
    lat_tiles = n_lat // tm
    return pl.pallas_call(
        functools.partial(_merge_kernel, tm=tm, n_lat=n_lat, alpha=alpha),
        grid=(bsz, n_tok // tm),
        in_specs=[
            pl.BlockSpec((1, tm, d), row),
            pl.BlockSpec((1, tm, d), lambda b, i: (b, jnp.minimum(i, lat_tiles - 1), 0)),
            pl.BlockSpec((1, tm, d), lambda b, i: (b, jnp.maximum(i - lat_tiles, 0), 0)),
            pl.BlockSpec((1, tm, d), row),
            pl.BlockSpec((1, tm, d), row),
            pl.BlockSpec((1, tm, COL_TILE), lambda b, i: (b, i, 5)),
            pl.BlockSpec((1, tm, COL_TILE), lambda b, i: (b, i, 6)),
            pl.BlockSpec((1, tm, COL_TILE), lambda b, i: (b, i, 7)),
            pl.BlockSpec((1, 8, d), lambda b, i: (b, 0, 0)),
            pl.BlockSpec((8, d), const),
            pl.BlockSpec((1, d), const),
            pl.BlockSpec((d, d), const),
            pl.BlockSpec((d, d), const),
            pl.BlockSpec((d, d), const),
            pl.BlockSpec((8, d), const),
        ],
        out_specs=[pl.BlockSpec((1, tm, d), row), pl.BlockSpec((1, tm, d), row)],
        out_shape=[out_sd, out_sd],
        compiler_params=_cparams(2),
        name="merge",
    )(xc, oa_lat, oa_ctx, hf, hb, z, z, z, mod_lat, mod_ctx, norm_m, w_a, w_m, w_o, ln1)


def _router_kernel(h_ref, wr_ref, eb_ref, idx_ref, wn_ref, pos_ref, cnt_ref, carry_scr, *, tm):
    @pl.when(pl.program_id(0) == 0)
    def _():
        carry_scr[...] = jnp.zeros_like(carry_scr)

    scores = _sigmoid(_dot3(h_ref[...], wr_ref[...]))
    biased = scores + eb_ref[...]
    lane = lax.broadcasted_iota(jnp.int32, (tm, N_EXPERTS), 1).astype(F32)
    hots, sels, ws = [], [], []
    for _ in range(TOP_K):
        mx = jnp.max(biased, axis=-1, keepdims=True)
        sel = jnp.min(jnp.where(biased == mx, lane, float(N_EXPERTS)), axis=-1, keepdims=True)
        hot = lane == sel
        hots.append(hot)
        sels.append(sel)
        ws.append(jnp.sum(jnp.where(hot, scores, 0.0), axis=-1, keepdims=True))
        biased = jnp.where(hot, -jnp.inf, biased)
    wsum = ws[0]
    chosen = hots[0]
    for r in range(1, TOP_K):
        wsum = wsum + ws[r]
        chosen = chosen | hots[r]
    chosen_f = jnp.where(chosen, 1.0, 0.0)
    ti = lax.broadcasted_iota(jnp.int32, (tm, tm), 0)
    si = lax.broadcasted_iota(jnp.int32, (tm, tm), 1)
    before = jnp.where(si < ti, 1.0, 0.0).astype(BF16)
    rank = carry_scr[0:1, :] + _dot(before, chosen_f.astype(BF16))
    out_lane = lax.broadcasted_iota(jnp.int32, (tm, 128), 1)
    idx_o = jnp.zeros((tm, 128), jnp.int32)
    pos_o = jnp.zeros((tm, 128), jnp.int32)
    wn_o = jnp.zeros((tm, 128), F32)
    for r in range(TOP_K):
        pos_r = jnp.sum(jnp.where(hots[r], rank, 0.0), axis=-1, keepdims=True).astype(jnp.int32)
        idx_o = jnp.where(out_lane == r, sels[r].astype(jnp.int32), idx_o)
        pos_o = jnp.where(out_lane == r, pos_r, pos_o)
        wn_o = jnp.where(out_lane == r, ws[r] / wsum * ROUTE_SCALE, wn_o)
    idx_ref[...] = idx_o
    pos_ref[...] = pos_o
    wn_ref[...] = wn_o
    total = carry_scr[0:1, :] + jnp.sum(chosen_f, axis=0, keepdims=True)
    carry_scr[...] = jnp.broadcast_to(total, carry_scr.shape)
    cnt_ref[...] = jnp.broadcast_to(total, cnt_ref.shape).astype(jnp.int32)


def _router(h2, w_router, e_bias):
    n, d = h2.shape
    tm = TOK_TILE
    row = lambda i: (i, 0)
    const = lambda i: (0, 0)
    return pl.pallas_call(
        functools.partial(_router_kernel, tm=tm),
        grid=(n // tm,),
        in_specs=[pl.BlockSpec((tm, d), row), pl.BlockSpec((d, N_EXPERTS), const),
                  pl.BlockSpec((1, N_EXPERTS), const)],
        out_specs=[pl.BlockSpec((tm, 128), row), pl.BlockSpec((tm, 128), row), pl.BlockSpec((tm, 128), row),
                   pl.BlockSpec((8, N_EXPERTS), const)],
        out_shape=[jax.ShapeDtypeStruct((n, 128), jnp.int32), jax.ShapeDtypeStruct((n, 128), F32),
                   jax.ShapeDtypeStruct((n, 128), jnp.int32), jax.ShapeDtypeStruct((8, N_EXPERTS), jnp.int32)],
        scratch_shapes=[pltpu.VMEM((8, N_EXPERTS), F32)],
        compiler_params=_cparams(1),
        name="router",
    )(h2, w_router, e_bias)


def _slots_kernel(idx_ref, pos_ref, start_ref, dest_ref, *, tm):
    lane = lax.broadcasted_iota(jnp.int32, (tm, N_EXPERTS), 1)
    out_lane = lax.broadcasted_iota(jnp.int32, (tm, 128), 1)
    idx = idx_ref[...]
    pos = pos_ref[...]
    start = start_ref[0:1, :]
    dest = jnp.zeros((tm, 128), F32)
    for r in range(TOP_K):
        base = jnp.sum(jnp.where(lane == idx[:, r:r + 1], start, 0.0), axis=-1, keepdims=True)
        dest = jnp.where(out_lane == r, base + pos[:, r:r + 1].astype(F32), dest)
    dest_ref[0] = dest.T[0:TOP_K, :].astype(jnp.int32)


def _slots(idx, pos, start):
    n = idx.shape[0]
    tm = TOK_TILE
    return pl.pallas_call(
        functools.partial(_slots_kernel, tm=tm),
        grid=(n // tm,),
        in_specs=[pl.BlockSpec((tm, 128), lambda i: (i, 0)), pl.BlockSpec((tm, 128), lambda i: (i, 0)),
                  pl.BlockSpec((8, N_EXPERTS), lambda i: (0, 0))],
        out_specs=pl.BlockSpec((1, TOP_K, tm), lambda i: (i, 0, 0)),
        out_shape=jax.ShapeDtypeStruct((n // tm, TOP_K, tm), jnp.int32),
        compiler_params=_cparams(1),
        name="slots",
    )(idx, pos, start).reshape(-1)


def _dispatch_kernel(dest_ref, h_ref, xs_in_ref, xs_ref, pk_scr, sem, *, tm):
    del xs_in_ref
    pk_scr[...] = _pack_bf16(h_ref[...])

    def start(r, c):
        for k in range(TOP_K):
            pltpu.make_async_copy(pk_scr.at[pl.ds(r, 1), :], xs_ref.at[pl.ds(dest_ref[k * tm + r], 1), :],
                                  sem).start(priority=k % 2)
        return c

    lax.fori_loop(0, tm, start, 0)
    for k in range(TOP_K):
        pltpu.make_async_copy(pk_scr, xs_ref.at[pl.ds(0, tm), :], sem).wait()


def _dispatch(dest, h2, xs_init):
    n, d = h2.shape
    tm = TOK_TILE
    return pl.pallas_call(
        functools.partial(_dispatch_kernel, tm=tm),
        grid=(n // tm,),
        in_specs=[pl.BlockSpec((TOP_K * tm,), lambda i: (i,), memory_space=pltpu.SMEM),
                  pl.BlockSpec((tm, d), lambda i: (i, 0)),
                  pl.BlockSpec(memory_space=pl.ANY)],
        out_specs=pl.BlockSpec(memory_space=pl.ANY),
        out_shape=jax.ShapeDtypeStruct(xs_init.shape, xs_init.dtype),
        scratch_shapes=[pltpu.VMEM((tm, d // 2), jnp.uint32), pltpu.SemaphoreType.DMA(())],
        input_output_aliases={2: 0},
        compiler_params=_cparams(1),
        name="dispatch",
    )(dest, h2, xs_init)


def _expert_kernel(blk_e_ref, n_used_ref, xs_ref, wi_ref, wo_ref, y_ref, wi_scr, wo_scr):
    i = pl.program_id(0)
    used = i < n_used_ref[0]

    @pl.when(used & ((i == 0) | (blk_e_ref[i] != blk_e_ref[jnp.maximum(i - 1, 0)])))
    def _():
        wi_scr[...] = wi_ref[0, 0].astype(BF16)
        wo_scr[...] = wo_ref[0, 0].astype(BF16)

    @pl.when(used)
    def _():
        x = jnp.concatenate(_unpack_bf16(xs_ref[...]), axis=1).astype(BF16)
        ag = _dot(x, wi_scr[...])
        a = ag[:, :E_HIDDEN]
        g = ag[:, E_HIDDEN:]
        y_ref[...] = _pack_bf16(_dot((a * _sigmoid(a) * g).astype(BF16), wo_scr[...]))

    @pl.when(jnp.logical_not(used))
    def _():
        y_ref[...] = jnp.zeros_like(y_ref)


def _experts(blk_e, n_used, xs, w_e_in, w_e_out, layer):
    n_slots, dp = xs.shape
    d = 2 * dp
    n_blk = n_slots // MOE_ROWS

    def blk(i, be, nu):
        return (jnp.minimum(i, nu[0] - 1), 0)

    def wsel(i, be, nu):
        return (layer, be[jnp.minimum(i, nu[0] - 1)], 0, 0)

    return pl.pallas_call(
        _expert_kernel,
        grid_spec=pltpu.PrefetchScalarGridSpec(
            num_scalar_prefetch=2,
            grid=(n_blk,),
            in_specs=[pl.BlockSpec((MOE_ROWS, dp), blk),
                      pl.BlockSpec((1, 1, d, 2 * E_HIDDEN), wsel),
                      pl.BlockSpec((1, 1, E_HIDDEN, d), wsel)],
            out_specs=pl.BlockSpec((MOE_ROWS, dp), lambda i, be, nu: (i, 0)),
            scratch_shapes=[pltpu.VMEM((d, 2 * E_HIDDEN), BF16), pltpu.VMEM((E_HIDDEN, d), BF16)],
        ),
        out_shape=jax.ShapeDtypeStruct((n_slots, dp), jnp.uint32),
        compiler_params=_cparams(1),
        name="experts",
    )(blk_e, n_used, xs, w_e_in, w_e_out)


def _combine_kernel(dest_ref, y_ref, h_ref, x1_ref, wn_ref, ml_ref, mc_ref, wsi_ref, wso_ref, ln_ref,
                    x2_ref, g_scr, sem, *, tm, n_lat, n_tok, alpha):
    def start(r, c):
        for k in range(TOP_K):
            pltpu.make_async_copy(y_ref.at[pl.ds(dest_ref[k * tm + r], 1), :], g_scr.at[k, pl.ds(r, 1), :],
                                  sem).start(priority=k % 2)
        return c

    lax.fori_loop(0, tm, start, 0)
    h = h_ref[...]
    ag = _dot(h.astype(BF16), wsi_ref[...])
    a = ag[:, :S_HIDDEN]
    g = ag[:, S_HIDDEN:]
    f = _dot((a * _sigmoid(a) * g).astype(BF16), wso_ref[...])
    wn = wn_ref[...]
    f_hi = f[:, :D_MODEL // 2]
    f_lo = f[:, D_MODEL // 2:]
    for k in range(TOP_K):
        pltpu.make_async_copy(y_ref.at[pl.ds(0, tm), :], g_scr.at[k], sem).wait()
    for k in range(TOP_K):
        y_hi, y_lo = _unpack_bf16(g_scr[k])
        f_hi = f_hi + wn[:, k:k + 1] * y_hi
        f_lo = f_lo + wn[:, k:k + 1] * y_lo
    f = jnp.concatenate([f_hi, f_lo], axis=1)
    rows = (pl.program_id(0) * tm) % n_tok + lax.broadcasted_iota(jnp.int32, (tm, 1), 0)
    is_ctx = rows >= n_lat
    u = alpha * x1_ref[...] + _pick_mod(is_ctx, mc_ref, ml_ref, 5) * f
    x2_ref[...] = _layer_norm(u, ln_ref[0:1, :], ln_ref[1:2, :])


def _combine(dest, y, h2, x1, wn, mod_lat, mod_ctx, ws_in, ws_out, ln2, *, n_lat, n_tok, alpha):
    n, d = h2.shape
    tm = TOK_TILE
    tiles_per_batch = n_tok // tm
    row = lambda i: (i, 0)
    const = lambda i: (0, 0)
    return pl.pallas_call(
        functools.partial(_combine_kernel, tm=tm, n_lat=n_lat, n_tok=n_tok, alpha=alpha),
        grid=(n // tm,),
        in_specs=[pl.BlockSpec((TOP_K * tm,), lambda i: (i,), memory_space=pltpu.SMEM),
                  pl.BlockSpec(memory_space=pl.ANY),
                  pl.BlockSpec((tm, d), row),
                  pl.BlockSpec((tm, d), row),
                  pl.BlockSpec((tm, 128), row),
                  pl.BlockSpec((1, 8, d), lambda i: (i // tiles_per_batch, 0, 0)),
                  pl.BlockSpec((8, d), const),
                  pl.BlockSpec((d, 2 * S_HIDDEN), const),
                  pl.BlockSpec((S_HIDDEN, d), const),
                  pl.BlockSpec((8, d), const)],
        out_specs=pl.BlockSpec((tm, d), row),
        out_shape=jax.ShapeDtypeStruct((n, d), F32),
        scratch_shapes=[pltpu.VMEM((TOP_K, tm, d // 2), jnp.uint32), pltpu.SemaphoreType.DMA(())],
        compiler_params=_cparams(1),
        name="combine",
    )(dest, y, h2, x1, wn, mod_lat, mod_ctx, ws_in, ws_out, ln2)


def _rope_tables(n_lat, n_ctx):
    rows = n_lat // GRID_W
    row = jnp.repeat(jnp.arange(rows, dtype=F32), GRID_W)
    col = jnp.tile(jnp.arange(GRID_W, dtype=F32), rows)
    n_freq = A_DIM // 4
    inv = jnp.power(ROPE_THETA, -jnp.arange(n_freq, dtype=F32) / n_freq)
    ang_r = row[:, None] * inv
    ang_c = col[:, None] * inv
    cos = jnp.concatenate([jnp.cos(ang_r), jnp.cos(ang_r), jnp.cos(ang_c), jnp.cos(ang_c)], axis=1)
    sin = jnp.concatenate([-jnp.sin(ang_r), jnp.sin(ang_r), -jnp.sin(ang_c), jnp.sin(ang_c)], axis=1)
    cos = jnp.concatenate([jnp.tile(cos, (1, 2)), jnp.ones((n_ctx, 128), F32)], axis=0)
    sin = jnp.concatenate([jnp.tile(sin, (1, 2)), jnp.zeros((n_ctx, 128), F32)], axis=0)
    return cos, sin


def _row_tile(n_tok):
    for tm in (1280, 640, 256):
        if n_tok % tm == 0:
            return tm
    raise ValueError(f"unsupported token count {n_tok}")


def _pad_rows(a, rows):
    return jnp.concatenate([a, jnp.zeros((rows - a.shape[0],) + a.shape[1:], a.dtype)], axis=0)


def kernel(x, c, ctx, c_ctx, ada_w, ada_b, w_in, b_gates, conv_qk, lam, subln_a, norm_m, w_br_a, w_br_m, w_out,
           ln1_g, ln1_b, ln2_g, ln2_b, w_router, e_bias, w_e_in, w_e_out, ws_in, ws_out):
    bsz, n_lat, d = x.shape
    n_ctx = ctx.shape[1]
    n_tok = n_lat + n_ctx
    depth = ada_w.shape[0]
    assert d == D_MODEL and n_ctx == MLSTM_CHUNK and n_lat % 512 == 0 and bsz + 1 <= 8
    alpha = (2 * depth) ** 0.25
    tm = _row_tile(n_tok)
    tq = 512
    ck = next(c for c in (1280, 256) if n_tok % c == 0)
    n_all = bsz * n_tok
    n_assign = n_all * TOP_K
    n_blk = -(-n_assign // MOE_ROWS) + N_EXPERTS
    n_slots = n_blk * MOE_ROWS

    cond = _pad_rows(jnp.concatenate([c, c_ctx[None, :]], axis=0), 8)
    mods = _ada(cond, ada_w, ada_b).reshape(depth, 8, N_MOD, d)
    cos_t, sin_t = _rope_tables(n_lat, n_ctx)
    xc = jnp.concatenate([x, ctx], axis=1)
    g_lo = 6 * COL_TILE
    g_hi = g_lo + 4 * M_HEADS

    for l in range(depth):
        lam_init = 0.8 - 0.6 * math.exp(-0.3 * l)
        mod_lat = _pad_rows(mods[l, :bsz].transpose(1, 0, 2), 8).transpose(1, 0, 2)
        mod_ctx = _pad_rows(mods[l, bsz], 8)
        w_main = jnp.concatenate([w_in[l, :, :g_lo], w_in[l, :, g_hi:]], axis=1).astype(BF16)
        w_gates = jnp.pad(w_in[l, :, g_lo:g_hi], ((0, 0), (0, 128 - 4 * M_HEADS)))
        bg = jnp.pad(b_gates[l], (0, 128 - 4 * M_HEADS))[None, :]
        conv_w = _pad_rows(conv_qk[l], 8)
        lam_p = jnp.pad(lam[l], ((0, 4), (0, 128 - A_DIM)))
        subln = subln_a[l][None, :]

        z, gates = _proj(xc, mod_lat, mod_ctx, w_main, w_gates, bg, cos_t, sin_t, conv_w, n_lat=n_lat, tm=tm)
        oa_lat = _attention(z, lam_p, subln, n_lat=n_lat, tq=tq, ck=ck, lat_queries=True, lam_init=lam_init)
        oa_ctx = _attention(z, lam_p, subln, n_lat=n_lat, tq=n_ctx, ck=ck, lat_queries=False, lam_init=lam_init)
        gates_t = gates[:, :, :4 * M_HEADS].transpose(0, 2, 1)
        hf, hb = _mlstm(z, gates, gates_t)
        ln1 = _pad_rows(jnp.stack([ln1_g[l], ln1_b[l]]), 8)
        x1, h2 = _merge(xc, oa_lat, oa_ctx, hf, hb, z, mod_lat, mod_ctx, norm_m[l][None, :], w_br_a[l].astype(BF16),
                        w_br_m[l].astype(BF16), w_out[l].astype(BF16), ln1, n_lat=n_lat, tm=TOK_TILE, alpha=alpha)

        h2f = h2.reshape(n_all, d)
        idx, wn, pos, cnt = _router(h2f, w_router[l], e_bias[l][None, :])
        counts = cnt[0]
        padded = (counts + MOE_ROWS - 1) // MOE_ROWS * MOE_ROWS
        pend = jnp.cumsum(padded)
        pstart = pend - padded
        dest = _slots(idx, pos, jnp.broadcast_to(pstart.astype(F32)[None, :], (8, N_EXPERTS)))
        blk_row = jnp.arange(n_blk, dtype=jnp.int32) * MOE_ROWS
        blk_e = jnp.minimum(jnp.sum(pend[None, :] <= blk_row[:, None], axis=1), N_EXPERTS - 1).astype(jnp.int32)
        n_used = (pend[-1:] // MOE_ROWS).astype(jnp.int32)
        xs = _dispatch(dest, h2f, jnp.zeros((n_slots, d // 2), jnp.uint32))
        y = _experts(blk_e, n_used, xs, w_e_in, w_e_out, l)
        ln2 = _pad_rows(jnp.stack([ln2_g[l], ln2_b[l]]), 8)
        x2 = _combine(dest, y, h2f, x1.reshape(n_all, d), wn, mod_lat, mod_ctx, ws_in[l].astype(BF16),
                      ws_out[l].astype(BF16), ln2, n_lat=n_lat, n_tok=n_tok, alpha=alpha)
        xc = x2.reshape(bsz, n_tok, d)
    return xc[:, :n_lat]
```
